```python
import math
import jax, jax.numpy as jnp
from jax import lax
import numpy as np

D_MODEL = 1024
BATCH = 2
SEQ = 8192
DEPTH = 1
DEC_BATCH = 128
DEC_SEQ = 8
PAST_LEN = 16384
PAGE_SIZE = 128

MIX_WIDTH = D_MODEL
A_WIDTH = MIX_WIDTH // 2
A_HEADS = 4
A_HALF = A_WIDTH // A_HEADS // 2
A_DV = 2 * A_HALF
B_WIDTH = MIX_WIDTH - A_WIDTH
B_HEADS = 4
B_DV = B_WIDTH // B_HEADS
B_DN = 128
B_DR = 64
Q_LORA = 3 * D_MODEL // 8
KV_LORA = D_MODEL // 4
MLA_ROW = KV_LORA + B_DR
MLA_SCALE = (B_DN + B_DR) ** -0.5
ROPE_THETA = 10000.0
N_IN = 3 * A_WIDTH + Q_LORA + KV_LORA + B_DR
REL_BUCKETS = 32
REL_MAX_DIST = 128
REL_EXACT = REL_BUCKETS // 2
PEER_HEADS = 8
PEER_DK = 256
N_KEYS = 128
N_EXPERTS = N_KEYS * N_KEYS
PEER_TOPK = 16
PEER_BLOCK = 128
Q_BLOCK = 128
ALPHA = (2 * DEPTH) ** 0.25
BETA = (8 * DEPTH) ** -0.25
NEG = -1e30

kernel_name = "hymba_diffattn_mla_peer_deepnorm_step"


def rms_norm(x, g, eps=1e-6):
    xf = x.astype(jnp.float32)
    y = xf * lax.rsqrt(jnp.mean(xf * xf, axis=-1, keepdims=True) + eps) * g.astype(jnp.float32)
    return y.astype(x.dtype)


def layer_norm(x, g, b, eps=1e-5):
    xf = x.astype(jnp.float32)
    xc = xf - jnp.mean(xf, axis=-1, keepdims=True)
    var = jnp.mean(xc * xc, axis=-1, keepdims=True)
    return (xc * lax.rsqrt(var + eps) * g.astype(jnp.float32) + b.astype(jnp.float32)).astype(x.dtype)


def t5_bucket(rel):
    n = jnp.maximum(rel, 0)
    large = REL_EXACT + (jnp.log(jnp.maximum(n, 1).astype(jnp.float32) / REL_EXACT)
                         / math.log(REL_MAX_DIST / REL_EXACT) * (REL_BUCKETS - REL_EXACT)).astype(jnp.int32)
    large = jnp.minimum(large, REL_BUCKETS - 1)
    return jnp.where(n < REL_EXACT, n, large)


def rope(x, pos):
    inv = 1.0 / (ROPE_THETA ** (jnp.arange(0, B_DR, 2, dtype=jnp.float32) / B_DR))
    ang = pos.astype(jnp.float32)[:, None] * inv[None, :]
    ang = ang.reshape(ang.shape[:1] + (1,) * (x.ndim - 3) + ang.shape[1:])
    cos, sin = jnp.cos(ang), jnp.sin(ang)
    xf = x.astype(jnp.float32)
    x1, x2 = xf[..., :B_DR // 2], xf[..., B_DR // 2:]
    return jnp.concatenate([x1 * cos - x2 * sin, x1 * sin + x2 * cos], axis=-1).astype(x.dtype)


def project(x, pos, w_in, q_norm_g, w_uq, kv_norm_g, w_uk):
    z = jnp.einsum('bld,dn->bln', x, w_in)
    shp = x.shape[:2] + (A_HEADS, A_DV)
    dq = z[..., :A_WIDTH].reshape(shp)
    dk = z[..., A_WIDTH:2 * A_WIDTH].reshape(shp)
    dv = z[..., 2 * A_WIDTH:3 * A_WIDTH].reshape(shp)
    o = 3 * A_WIDTH
    cq = z[..., o:o + Q_LORA]
    o += Q_LORA
    ckv = z[..., o:o + KV_LORA]
    o += KV_LORA
    kr = z[..., o:]
    q = jnp.einsum('blr,rhe->blhe', rms_norm(cq, q_norm_g), w_uq)
    q_lat = jnp.einsum('blhd,rhd->blhr', q[..., :B_DN], w_uk)
    q_rope = rope(q[..., B_DN:], pos)
    return dq, dk, dv, q_lat, q_rope, rms_norm(ckv, kv_norm_g), rope(kr, pos)


def diff_core(q, k, v, qpos, kpos, rel_bias, lam, subln_g, lam_init):
    lq, lk = q.shape[0], k.shape[0]
    qh = q.reshape(lq, A_HEADS, 2, A_HALF)
    kh = k.reshape(lk, A_HEADS, 2, A_HALF)
    s = jnp.einsum('qhcd,khcd->chqk', qh, kh, preferred_element_type=jnp.float32) * (A_HALF ** -0.5)
    rel = qpos[:, None] - kpos[None, :]
    bias = jnp.transpose(rel_bias[t5_bucket(rel)].astype(jnp.float32), (2, 0, 1))
    s = jnp.where(rel >= 0, s + bias, NEG)
    p = jax.nn.softmax(s, axis=-1)
    w = p[0] - lam * p[1]
    o = jnp.einsum('hqk,khe->qhe', w, v.astype(jnp.float32))
    o = rms_norm(o, subln_g, 1e-5) * (1.0 - lam_init)
    return o.reshape(lq, A_WIDTH).astype(q.dtype)


def mla_core(q_lat, q_rope, ckv, kr, qpos, kpos, w_uv):
    s = (jnp.einsum('qhr,kr->hqk', q_lat, ckv, preferred_element_type=jnp.float32)
         + jnp.einsum('qhd,kd->hqk', q_rope, kr, preferred_element_type=jnp.float32)) * MLA_SCALE
    s = jnp.where(kpos[None, :] <= qpos[:, None], s, NEG)
    p = jax.nn.softmax(s, axis=-1)
    o_lat = jnp.einsum('hqk,kr->qhr', p, ckv.astype(jnp.float32))
    o = jnp.einsum('qhr,rhe->qhe', o_lat, w_uv.astype(jnp.float32))
    return o.reshape(q_lat.shape[0], B_WIDTH).astype(q_lat.dtype)


def peer_ffn(h, w_q, sub_keys, u_tab, v_tab):
    t = h.shape[0]
    nblk = -(-t // PEER_BLOCK)
    hb = jnp.pad(h, ((0, nblk * PEER_BLOCK - t), (0, 0))).reshape(nblk, PEER_BLOCK, D_MODEL)

    def block(hx):
        q = jnp.einsum('td,dn->tn', hx, w_q).reshape(PEER_BLOCK, PEER_HEADS, PEER_DK)
        s1 = jnp.einsum('thd,nd->thn', q[..., :PEER_DK // 2], sub_keys[0], preferred_element_type=jnp.float32)
        s2 = jnp.einsum('thd,nd->thn', q[..., PEER_DK // 2:], sub_keys[1], preferred_element_type=jnp.float32)
        v1, i1 = lax.top_k(s1, PEER_TOPK)
        v2, i2 = lax.top_k(s2, PEER_TOPK)
        cand = (v1[..., :, None] + v2[..., None, :]).reshape(PEER_BLOCK, PEER_HEADS, PEER_TOPK * PEER_TOPK)
        cidx = (i1[..., :, None] * N_KEYS + i2[..., None, :]).reshape(PEER_BLOCK, PEER_HEADS, PEER_TOPK * PEER_TOPK)
        best, sel = lax.top_k(cand, PEER_TOPK)
        idx = jnp.take_along_axis(cidx, sel, axis=-1)
        g = jax.nn.softmax(best, axis=-1)
        act = jax.nn.gelu(jnp.einsum('thkd,td->thk', u_tab[idx], hx, preferred_element_type=jnp.float32),
                          approximate=False)
        return jnp.einsum('thk,thkd->td', (g * act).astype(hx.dtype), v_tab[idx])

    return lax.map(block, hb).reshape(nblk * PEER_BLOCK, D_MODEL)[:t]


def post_layer(h, mix, w_out, ln1_g, ln1_b, peer_w_q, peer_sub_keys, peer_u, peer_v, ln2_g, ln2_b):
    a = jnp.einsum('ble,ed->bld', mix, w_out)
    h = layer_norm(ALPHA * h + a, ln1_g, ln1_b)
    p = peer_ffn(h.reshape(-1, D_MODEL), peer_w_q, peer_sub_keys, peer_u, peer_v).reshape(h.shape)
    return layer_norm(ALPHA * h + p, ln2_g, ln2_b)


def setup_inputs(seed: int = 0) -> dict:
    key = jax.random.key(seed)
    ks = jax.random.split(key, 24)
    n_pages = PAST_LEN // PAGE_SIZE
    n_used = DEC_BATCH * n_pages
    n_pool = n_used + max(1, n_used // 4)

    def nrm(k, shape, scale):
        return jax.random.normal(k, shape, jnp.float32) * scale

    w_in = jnp.concatenate([
        nrm(ks[4], (DEPTH, D_MODEL, 2 * A_WIDTH), D_MODEL ** -0.5),
        nrm(ks[5], (DEPTH, D_MODEL, A_WIDTH), BETA * D_MODEL ** -0.5),
        nrm(ks[6], (DEPTH, D_MODEL, Q_LORA + KV_LORA + B_DR), D_MODEL ** -0.5)], axis=-1)
    page_table = jax.random.permutation(ks[3], n_pool)[:n_used].reshape(DEC_BATCH, n_pages).astype(jnp.int32)
    return {
        "x_prompt": nrm(ks[0], (BATCH, SEQ, D_MODEL), 1.0),
        "x_sample": nrm(ks[1], (DEC_BATCH, DEC_SEQ, D_MODEL), 1.0),
        "cache_diff_kv": nrm(ks[2], (DEPTH, n_pool, PAGE_SIZE, 2, A_HEADS, A_DV), 1.0),
        "cache_mla": nrm(ks[7], (DEPTH, n_pool, PAGE_SIZE, MLA_ROW), 1.0),
        "page_table": page_table,
        "w_in": w_in,
        "diff_lambda": nrm(ks[8], (DEPTH, 4, A_HALF), 0.1),
        "diff_subln_g": 1.0 + nrm(ks[9], (DEPTH, A_DV), 0.02),
        "rel_bias": nrm(ks[10], (REL_BUCKETS, A_HEADS), 0.5),
        "mla_q_norm_g": 1.0 + nrm(ks[11], (DEPTH, Q_LORA), 0.02),
        "mla_w_uq": nrm(ks[12], (DEPTH, Q_LORA, B_HEADS, B_DN + B_DR), Q_LORA ** -0.5),
        "mla_kv_norm_g": 1.0 + nrm(ks[13], (DEPTH, KV_LORA), 0.02),
        "mla_w_uk": nrm(ks[14], (DEPTH, KV_LORA, B_HEADS, B_DN), KV_LORA ** -0.5),
        "mla_w_uv": nrm(ks[15], (DEPTH, KV_LORA, B_HEADS, B_DV), BETA * KV_LORA ** -0.5),
        "w_out": nrm(ks[16], (DEPTH, MIX_WIDTH, D_MODEL), BETA * MIX_WIDTH ** -0.5),
        "ln1_g": 1.0 + nrm(ks[17], (DEPTH, D_MODEL), 0.02),
        "ln1_b": nrm(ks[18], (DEPTH, D_MODEL), 0.02),
        "peer_w_q": nrm(ks[19], (DEPTH, D_MODEL, PEER_HEADS * PEER_DK), D_MODEL ** -0.5),
        "peer_sub_keys": nrm(ks[20], (DEPTH, 2, N_KEYS, PEER_DK // 2), (PEER_DK // 2) ** -0.5),
        "peer_u": nrm(ks[21], (DEPTH, N_EXPERTS, D_MODEL), D_MODEL ** -0.5),
        "peer_v": nrm(ks[22], (DEPTH, N_EXPERTS, D_MODEL), BETA),
        "ln2_g": 1.0 + nrm(ks[23], (DEPTH, D_MODEL), 0.02),
        "ln2_b": nrm(jax.random.fold_in(ks[23], 1), (DEPTH, D_MODEL), 0.02),
    }


def reference(x_prompt, x_sample, cache_diff_kv, cache_mla, page_table, w_in, diff_lambda, diff_subln_g,
              rel_bias, mla_q_norm_g, mla_w_uq, mla_kv_norm_g, mla_w_uk, mla_w_uv, w_out, ln1_g, ln1_b,
              peer_w_q, peer_sub_keys, peer_u, peer_v, ln2_g, ln2_b):
    seq = x_prompt.shape[1]
    dec_seq = x_sample.shape[1]
    past = page_table.shape[1] * cache_mla.shape[2]
    pos_p = jnp.arange(seq, dtype=jnp.int32)
    pos_s = past + jnp.arange(dec_seq, dtype=jnp.int32)
    kpos_s = jnp.arange(past + dec_seq, dtype=jnp.int32)
    n_blk = seq // Q_BLOCK

    hp, hs = x_prompt, x_sample
    kv_rows_p, mla_rows_p, kv_rows_s, mla_rows_s = [], [], [], []
    for l in range(DEPTH):
        lam_init = 0.8 - 0.6 * math.exp(-0.3 * l)
        lp = diff_lambda[l].astype(jnp.float32)
        lam = jnp.exp(jnp.sum(lp[0] * lp[1])) - jnp.exp(jnp.sum(lp[2] * lp[3])) + lam_init
        sub_g = diff_subln_g[l]
        wuv = mla_w_uv[l]
        post_args = (w_out[l], ln1_g[l], ln1_b[l], peer_w_q[l], peer_sub_keys[l], peer_u[l], peer_v[l],
                     ln2_g[l], ln2_b[l])

        dq, dk, dv, ql, qr, ckv, kr = project(hp, pos_p, w_in[l], mla_q_norm_g[l], mla_w_uq[l],
                                              mla_kv_norm_g[l], mla_w_uk[l])

        def prompt_seq(dq, dk, dv, ql, qr, ckv, kr):
            def block(a):
                dq_b, ql_b, qr_b, p_b = a
                return jnp.concatenate([
                    diff_core(dq_b, dk, dv, p_b, pos_p, rel_bias, lam, sub_g, lam_init),
                    mla_core(ql_b, qr_b, ckv, kr, p_b, pos_p, wuv)], axis=-1)
            blocks = (dq.reshape((n_blk, Q_BLOCK) + dq.shape[1:]),
                      ql.reshape((n_blk, Q_BLOCK) + ql.shape[1:]),
                      qr.reshape((n_blk, Q_BLOCK) + qr.shape[1:]),
                      pos_p.reshape(n_blk, Q_BLOCK))
            return lax.map(block, blocks).reshape(seq, MIX_WIDTH)

        mix_p = jax.vmap(prompt_seq)(dq, dk, dv, ql, qr, ckv, kr)
        kv_rows_p.append(jnp.stack([dk, dv], axis=2))
        mla_rows_p.append(jnp.concatenate([ckv, kr], axis=-1))
        hp = post_layer(hp, mix_p, *post_args)

        dq, dk, dv, ql, qr, ckv, kr = project(hs, pos_s, w_in[l], mla_q_norm_g[l], mla_w_uq[l],
                                              mla_kv_norm_g[l], mla_w_uk[l])

        def sample_seq(a):
            pages, dq_i, dk_i, dv_i, ql_i, qr_i, ckv_i, kr_i = a
            kv = cache_diff_kv[l, pages].reshape(past, 2, A_HEADS, A_DV)
            row = cache_mla[l, pages].reshape(past, MLA_ROW)
            k = jnp.concatenate([kv[:, 0], dk_i], axis=0)
            v = jnp.concatenate([kv[:, 1], dv_i], axis=0)
            c = jnp.concatenate([row[:, :KV_LORA], ckv_i], axis=0)
            r = jnp.concatenate([row[:, KV_LORA:], kr_i], axis=0)
            return jnp.concatenate([
                diff_core(dq_i, k, v, pos_s, kpos_s, rel_bias, lam, sub_g, lam_init),
                mla_core(ql_i, qr_i, c, r, pos_s, kpos_s, wuv)], axis=-1)

        mix_s = lax.map(sample_seq, (page_table, dq, dk, dv, ql, qr, ckv, kr))
        kv_rows_s.append(jnp.stack([dk, dv], axis=2))
        mla_rows_s.append(jnp.concatenate([ckv, kr], axis=-1))
        hs = post_layer(hs, mix_s, *post_args)

    return (hp, hs, jnp.stack(kv_rows_p), jnp.stack(mla_rows_p), jnp.stack(kv_rows_s), jnp.stack(mla_rows_s))
```

```python
import functools
import math

import numpy as np
import jax
import jax.numpy as jnp
from jax import lax
from jax.experimental import pallas as pl
from jax.experimental.pallas import tpu as pltpu

F32 = jnp.float32
BF16 = jnp.bfloat16

A_HEADS = 4
A_HALF = 64
A_DV = 128
A_WIDTH = A_HEADS * A_DV
B_HEADS = 4
B_DV = 128
B_DN = 128
B_DR = 64
ROPE_THETA = 10000.0
REL_BUCKETS = 32
REL_MAX_DIST = 128
REL_EXACT = REL_BUCKETS // 2
PEER_HEADS = 8
PEER_DK = 256
N_KEYS = 128
PEER_TOPK = 16
NEG = -1e30

LANES = 128
MLA_PAD = 384

VMEM_LIMIT = 48 * 1024 * 1024


def _cparams(sem):
    return pltpu.CompilerParams(dimension_semantics=sem, vmem_limit_bytes=VMEM_LIMIT)


def _full(shape):
    n = len(shape)
    return pl.BlockSpec(shape, lambda *_: (0,) * n)


def _t5_bucket(rel):
    n = jnp.maximum(rel, 0)
    large = REL_EXACT + (jnp.log(jnp.maximum(n, 1).astype(F32) / REL_EXACT)
                         / math.log(REL_MAX_DIST / REL_EXACT) * (REL_BUCKETS - REL_EXACT)).astype(jnp.int32)
    large = jnp.minimum(large, REL_BUCKETS - 1)
    return jnp.where(n < REL_EXACT, n, large)


def _nt(a, b):
    return lax.dot_general(a, b, (((1,), (1,)), ((), ())), preferred_element_type=F32)


def _tn(a, b):
    return lax.dot_general(a, b, (((0,), (0,)), ((), ())), preferred_element_type=F32)


def _dot(a, b):
    return jnp.dot(a, b, preferred_element_type=F32)


def _project_kernel(x_ref, win_ref, qg_ref, wuq_ref, kvg_ref, wuk_ref, cos_ref, sin_ref,
                    kv_out, mla_out, qd_out, kd_out, vd_out, qm_out, km_out, *, a_scale, mla_scale,
                    q_lora, kv_lora):
    x = x_ref[...].astype(BF16)
    z = _dot(x, win_ref[...])
    aw = A_WIDTH
    kv_out[...] = z[:, aw:3 * aw]
    qd_out[...] = (z[:, :aw] * a_scale).astype(BF16)
    kd_out[...] = z[:, aw:2 * aw].astype(BF16)
    vd_out[...] = z[:, 2 * aw:3 * aw].astype(BF16)
    o = 3 * aw
    cq = z[:, o:o + q_lora]
    o += q_lora
    ckv = z[:, o:o + kv_lora]
    o += kv_lora
    kr = z[:, o:o + LANES]
    krs = z[:, o + LANES:o + 2 * LANES]
    cos = cos_ref[...]
    sin = sin_ref[...]

    cqn = cq * lax.rsqrt(jnp.mean(cq * cq, axis=-1, keepdims=True) + 1e-6) * qg_ref[...]
    q = _dot(cqn.astype(BF16), wuq_ref[...])
    ckvn = ckv * lax.rsqrt(jnp.mean(ckv * ckv, axis=-1, keepdims=True) + 1e-6) * kvg_ref[...]
    krr = kr * cos + krs * sin

    mla_out[:, :kv_lora] = ckvn
    mla_out[:, kv_lora:] = krr[:, :B_DR]
    km_out[:, :kv_lora] = ckvn.astype(BF16)
    km_out[:, kv_lora:] = krr.astype(BF16)

    nope_w = B_HEADS * B_DN
    rope_w = B_HEADS * LANES
    for h in range(B_HEADS):
        ql = _dot(q[:, h * B_DN:(h + 1) * B_DN].astype(BF16), wuk_ref[h])
        qr = (q[:, nope_w + h * LANES:nope_w + (h + 1) * LANES] * cos
              + q[:, nope_w + rope_w + h * LANES:nope_w + rope_w + (h + 1) * LANES] * sin)
        base = h * MLA_PAD
        qm_out[:, base:base + kv_lora] = (ql * mla_scale).astype(BF16)
        qm_out[:, base + kv_lora:base + MLA_PAD] = (qr * mla_scale).astype(BF16)


def _project(x2d, cos, sin, win, qg, wuq, kvg, wuk, *, tm, q_lora, kv_lora):
    t, d = x2d.shape
    aw = A_WIDTH
    mla_row = kv_lora + B_DR
    kern = functools.partial(_project_kernel, a_scale=A_HALF ** -0.5, mla_scale=(B_DN + B_DR) ** -0.5,
                             q_lora=q_lora, kv_lora=kv_lora)
    row = lambda w: pl.BlockSpec((tm, w), lambda i: (i, 0))
    return pl.pallas_call(
        kern,
        grid=(t // tm,),
        in_specs=[row(d), _full(win.shape), _full(qg.shape), _full(wuq.shape), _full(kvg.shape),
                  _full(wuk.shape), row(LANES), row(LANES)],
        out_specs=[row(2 * aw), row(mla_row), row(aw), row(aw), row(aw), row(B_HEADS * MLA_PAD), row(MLA_PAD)],
        out_shape=[jax.ShapeDtypeStruct((t, 2 * aw), F32), jax.ShapeDtypeStruct((t, mla_row), F32),
                   jax.ShapeDtypeStruct((t, aw), BF16), jax.ShapeDtypeStruct((t, aw), BF16),
                   jax.ShapeDtypeStruct((t, aw), BF16), jax.ShapeDtypeStruct((t, B_HEADS * MLA_PAD), BF16),
                   jax.ShapeDtypeStruct((t, MLA_PAD), BF16)],
        compiler_params=_cparams(("parallel",)),
        name="project",
    )(x2d, win, qg, wuq, kvg, wuk, cos, sin)


def _lambda_value(lam_ref, lam_init):
    lp = lam_ref[...]
    l01 = jnp.sum(lp[0:1] * lp[1:2], axis=-1, keepdims=True)
    l23 = jnp.sum(lp[2:3] * lp[3:4], axis=-1, keepdims=True)
    return jnp.exp(l01) - jnp.exp(l23) + lam_init


def _online_update(s, v, m_ref, l_ref, acc_ref, idx):
    m_prev = m_ref[idx]
    m_new = jnp.maximum(m_prev, jnp.max(s, axis=-1, keepdims=True))
    alpha = jnp.exp(m_prev - m_new)
    e = jnp.exp(s - m_new)
    l_ref[idx] = alpha * l_ref[idx] + jnp.sum(e, axis=-1, keepdims=True)
    acc_ref[idx] = alpha * acc_ref[idx] + _dot(e.astype(BF16), v)
    m_ref[idx] = m_new


def _diff_finish(o_a, o_b, lam, subg, lam_init):
    w = o_a - lam * o_b
    return w * lax.rsqrt(jnp.mean(w * w, axis=-1, keepdims=True) + 1e-5) * subg * (1.0 - lam_init)


def _prompt_attn_kernel(qi_ref, kj_ref, qd_ref, qm_ref, kd_ref, vd_ref, km_ref, bias_ref, lam_ref, subg_ref,
                        wuv_ref, o_ref, m_d, l_d, acc_d, m_m, l_m, acc_m, *, lam_init, kv_lora):
    p = pl.program_id(1)
    qi = qi_ref[p]
    kj = kj_ref[p]

    @pl.when(kj == 0)
    def _():
        m_d[...] = jnp.full(m_d.shape, -jnp.inf, F32)
        l_d[...] = jnp.zeros(l_d.shape, F32)
        acc_d[...] = jnp.zeros(acc_d.shape, F32)
        m_m[...] = jnp.full(m_m.shape, -jnp.inf, F32)
        l_m[...] = jnp.zeros(l_m.shape, F32)
        acc_m[...] = jnp.zeros(acc_m.shape, F32)

    bq = qd_ref.shape[0]
    lane = lax.broadcasted_iota(jnp.int32, (bq, A_DV), 1)
    for h in range(A_HEADS):
        qh = qd_ref[:, h * A_DV:(h + 1) * A_DV]
        kh = kd_ref[:, h * A_DV:(h + 1) * A_DV]
        vh = vd_ref[:, h * A_DV:(h + 1) * A_DV]
        zero = jnp.zeros_like(qh)
        bias = bias_ref[0, h]
        for c in range(2):
            qc = jnp.where((lane < A_HALF) if c == 0 else (lane >= A_HALF), qh, zero)
            s = _nt(qc, kh) + bias
            _online_update(s, vh, m_d, l_d, acc_d, 2 * h + c)

    km = km_ref[...]
    mask = bias_ref[0, A_HEADS]
    for h in range(B_HEADS):
        s = _nt(qm_ref[:, h * MLA_PAD:(h + 1) * MLA_PAD], km) + mask
        _online_update(s, km[:, :kv_lora], m_m, l_m, acc_m, h)

    @pl.when(kj == qi)
    def _():
        lam = _lambda_value(lam_ref, lam_init)
        for h in range(A_HEADS):
            o_a = acc_d[2 * h] / l_d[2 * h]
            o_b = acc_d[2 * h + 1] / l_d[2 * h + 1]
            o_ref[:, h * A_DV:(h + 1) * A_DV] = _diff_finish(o_a, o_b, lam, subg_ref[...], lam_init).astype(o_ref.dtype)
        for h in range(B_HEADS):
            ol = acc_m[h] / l_m[h]
            o = _dot(ol.astype(BF16), wuv_ref[h])
            o_ref[:, A_WIDTH + h * B_DV:A_WIDTH + (h + 1) * B_DV] = o.astype(o_ref.dtype)


def _prompt_attn(qd, qm, kd, vd, km, bias, lam_p, subg, wuv, *, batch, seq, blk, lam_init, kv_lora):
    nq = seq // blk
    pairs = [(i, j) for i in range(nq) for j in range(i + 1)]
    qi_arr = jnp.asarray(np.array([p[0] for p in pairs], np.int32))
    kj_arr = jnp.asarray(np.array([p[1] for p in pairs], np.int32))
    mix_w = A_WIDTH + B_HEADS * B_DV
    qmap = lambda b, p, qi, kj: (b * nq + qi[p], 0)
    kmap = lambda b, p, qi, kj: (b * nq + kj[p], 0)
    bmap = lambda b, p, qi, kj: (jnp.minimum(qi[p] - kj[p], 2), 0, 0, 0)
    cst = lambda shape: pl.BlockSpec(shape, lambda b, p, qi, kj: (0,) * len(shape))
    kern = functools.partial(_prompt_attn_kernel, lam_init=lam_init, kv_lora=kv_lora)
    grid_spec = pltpu.PrefetchScalarGridSpec(
        num_scalar_prefetch=2,
        grid=(batch, len(pairs)),
        in_specs=[pl.BlockSpec((blk, A_WIDTH), qmap), pl.BlockSpec((blk, B_HEADS * MLA_PAD), qmap),
                  pl.BlockSpec((blk, A_WIDTH), kmap), pl.BlockSpec((blk, A_WIDTH), kmap),
                  pl.BlockSpec((blk, MLA_PAD), kmap),
                  pl.BlockSpec((1, A_HEADS + 1, blk, blk), bmap),
                  cst(lam_p.shape), cst(subg.shape), cst(wuv.shape)],
        out_specs=pl.BlockSpec((blk, mix_w), qmap),
        scratch_shapes=[pltpu.VMEM((2 * A_HEADS, blk, 1), F32), pltpu.VMEM((2 * A_HEADS, blk, 1), F32),
                        pltpu.VMEM((2 * A_HEADS, blk, A_DV), F32),
                        pltpu.VMEM((B_HEADS, blk, 1), F32), pltpu.VMEM((B_HEADS, blk, 1), F32),
                        pltpu.VMEM((B_HEADS, blk, kv_lora), F32)],
    )
    return pl.pallas_call(
        kern,
        grid_spec=grid_spec,
        out_shape=jax.ShapeDtypeStruct((batch * seq, mix_w), BF16),
        compiler_params=_cparams(("parallel", "arbitrary")),
        name="prompt_attn",
    )(qi_arr, kj_arr, qd, qm, kd, vd, km, bias, lam_p, subg, wuv)


def _sample_attn_kernel(pt_ref, qd_ref, qm_ref, kdn_ref, vdn_ref, kmn_ref, bias_ref, biasn_ref, maskn_ref,
                        lam_ref, subg_ref, wuv_ref, *rest, lam_init, kv_lora, n_per_step):
    kv_refs = rest[:n_per_step]
    mla_refs = rest[n_per_step:2 * n_per_step]
    o_ref, qbd, qmm, m_d, l_d, acc_d, m_m, l_m, acc_m = rest[2 * n_per_step:]
    p = pl.program_id(1)
    dec = qd_ref.shape[1]
    n_maps = 2 * A_HEADS
    mla_row = kv_lora + B_DR

    @pl.when(p == 0)
    def _():
        q8 = qd_ref[0].astype(F32)
        q_rep = jnp.concatenate([q8] * n_maps, axis=0)
        row = lax.broadcasted_iota(jnp.int32, q_rep.shape, 0)
        lane = lax.broadcasted_iota(jnp.int32, q_rep.shape, 1)
        qbd[...] = jnp.where(lane // A_HALF == row // dec, q_rep, 0.0).astype(BF16)
        qm8 = qm_ref[0].astype(F32)
        qmm[...] = jnp.concatenate([qm8[:, h * MLA_PAD:h * MLA_PAD + mla_row] for h in range(B_HEADS)],
                                   axis=0).astype(BF16)
        m_d[...] = jnp.full(m_d.shape, -jnp.inf, F32)
        l_d[...] = jnp.zeros(l_d.shape, F32)
        acc_d[...] = jnp.zeros(acc_d.shape, F32)
        m_m[...] = jnp.full(m_m.shape, -jnp.inf, F32)
        l_m[...] = jnp.zeros(l_m.shape, F32)
        acc_m[...] = jnp.zeros(acc_m.shape, F32)

    q_d = qbd[...]
    q_m = qmm[...]
    for j in range(n_per_step):
        kv = kv_refs[j][0]
        k = kv[:, :A_WIDTH].astype(BF16)
        v = kv[:, A_WIDTH:].astype(BF16)
        s = _nt(q_d, k) + bias_ref[j]
        _online_update(s, v, m_d, l_d, acc_d, 0)
        row = mla_refs[j][0].astype(BF16)
        s = _nt(q_m, row)
        _online_update(s, row[:, :kv_lora], m_m, l_m, acc_m, 0)

    @pl.when(p == pl.num_programs(1) - 1)
    def _():
        s = _nt(q_d, kdn_ref[0]) + biasn_ref[...]
        _online_update(s, vdn_ref[0], m_d, l_d, acc_d, 0)
        kmn = kmn_ref[0]
        s = _nt(q_m, kmn[:, :mla_row]) + maskn_ref[...]
        _online_update(s, kmn[:, :kv_lora], m_m, l_m, acc_m, 0)

        lam = _lambda_value(lam_ref, lam_init)
        o_all = acc_d[0] / l_d[0]
        for h in range(A_HEADS):
            ra = (2 * h) * dec
            rb = (2 * h + 1) * dec
            o_a = o_all[ra:ra + dec, h * A_DV:(h + 1) * A_DV]
            o_b = o_all[rb:rb + dec, h * A_DV:(h + 1) * A_DV]
            o_ref[0, :, h * A_DV:(h + 1) * A_DV] = _diff_finish(o_a, o_b, lam, subg_ref[...], lam_init).astype(o_ref.dtype)
        ol_all = acc_m[0] / l_m[0]
        for h in range(B_HEADS):
            ol = ol_all[h * dec:(h + 1) * dec]
            o = _dot(ol.astype(BF16), wuv_ref[h])
            o_ref[0, :, A_WIDTH + h * B_DV:A_WIDTH + (h + 1) * B_DV] = o.astype(o_ref.dtype)


def _sample_attn(page_table, qd, qm, kdn, vdn, kmn, cache_kv, cache_mla, bias, biasn, maskn, lam_p, subg, wuv,
                 *, layer_off, lam_init, kv_lora, n_per_step):
    sb, dec, _ = qd.shape
    n_pages = page_table.shape[1]
    page = cache_mla.shape[1]
    mla_row = kv_lora + B_DR
    mix_w = A_WIDTH + B_HEADS * B_DV
    steps = n_pages // n_per_step
    rows_d = 2 * A_HEADS * dec
    rows_m = B_HEADS * dec
    pt_flat = page_table.reshape(-1)

    seq3 = lambda w: pl.BlockSpec((1, dec, w), lambda s, p, pt: (s, 0, 0))
    cst = lambda shape: pl.BlockSpec(shape, lambda s, p, pt: (0,) * len(shape))

    def page_map(j):
        return lambda s, p, pt: (pt[s * n_pages + p * n_per_step + j] + layer_off, 0, 0)

    in_specs = [seq3(A_WIDTH), seq3(B_HEADS * MLA_PAD), seq3(A_WIDTH), seq3(A_WIDTH), seq3(MLA_PAD),
                pl.BlockSpec((n_per_step, rows_d, page), lambda s, p, pt: (p, 0, 0)),
                cst(biasn.shape), cst(maskn.shape), cst(lam_p.shape), cst(subg.shape), cst(wuv.shape)]
    in_specs += [pl.BlockSpec((1, page, 2 * A_WIDTH), page_map(j)) for j in range(n_per_step)]
    in_specs += [pl.BlockSpec((1, page, mla_row), page_map(j)) for j in range(n_per_step)]
    kern = functools.partial(_sample_attn_kernel, lam_init=lam_init, kv_lora=kv_lora, n_per_step=n_per_step)
    grid_spec = pltpu.PrefetchScalarGridSpec(
        num_scalar_prefetch=1,
        grid=(sb, steps),
        in_specs=in_specs,
        out_specs=pl.BlockSpec((1, dec, mix_w), lambda s, p, pt: (s, 0, 0)),
        scratch_shapes=[pltpu.VMEM((rows_d, A_WIDTH), BF16), pltpu.VMEM((rows_m, mla_row), BF16),
                        pltpu.VMEM((1, rows_d, 1), F32), pltpu.VMEM((1, rows_d, 1), F32),
                        pltpu.VMEM((1, rows_d, A_WIDTH), F32),
                        pltpu.VMEM((1, rows_m, 1), F32), pltpu.VMEM((1, rows_m, 1), F32),
                        pltpu.VMEM((1, rows_m, kv_lora), F32)],
    )
    return pl.pallas_call(
        kern,
        grid_spec=grid_spec,
        out_shape=jax.ShapeDtypeStruct((sb, dec, mix_w), BF16),
        compiler_params=_cparams(("parallel", "arbitrary")),
        name="sample_attn",
    )(pt_flat, qd, qm, kdn, vdn, kmn, bias, biasn, maskn, lam_p, subg, wuv,
      *([cache_kv] * n_per_step), *([cache_mla] * n_per_step))


def _layer_norm(x, g, b):
    xc = x - jnp.mean(x, axis=-1, keepdims=True)
    var = jnp.mean(xc * xc, axis=-1, keepdims=True)
    return xc * lax.rsqrt(var + 1e-5) * g + b


def _extract_topk(s, k_top):
    n, t = s.shape
    iota = lax.broadcasted_iota(jnp.int32, (n, t), 0).astype(F32)
    iota_k = lax.broadcasted_iota(jnp.int32, (k_top, t), 0)
    rank = jnp.full((n, t), float(n), F32)
    vals = jnp.zeros((k_top, t), F32)
    for k in range(k_top):
        m = jnp.max(s, axis=0, keepdims=True)
        idx = jnp.min(jnp.where(s == m, iota, float(n)), axis=0, keepdims=True)
        sel = iota == idx
        rank = jnp.where(sel, float(k), rank)
        s = jnp.where(sel, -jnp.inf, s)
        vals = jnp.where(iota_k == k, m, vals)
    return rank, vals


def _post_select_kernel(mix_ref, x_ref, wout_ref, g_ref, b_ref, wq_ref, sk_ref,
                        h_out, hb_out, a_out, cnt_out, b_out, r2_out, *, alpha):
    a = _dot(mix_ref[...], wout_ref[...])
    h = _layer_norm(alpha * x_ref[...] + a, g_ref[...], b_ref[...])
    h_out[...] = h
    hb = h.astype(BF16)
    hb_out[...] = hb
    q = _dot(hb, wq_ref[...])
    half = PEER_DK // 2
    kk = PEER_TOPK
    for hd in range(PEER_HEADS):
        q1 = q[:, hd * PEER_DK:hd * PEER_DK + half].astype(BF16)
        q2 = q[:, hd * PEER_DK + half:(hd + 1) * PEER_DK].astype(BF16)
        s1 = _nt(sk_ref[0], q1)
        s2 = _nt(sk_ref[1], q2)
        r1, v1 = _extract_topk(s1, kk)
        r2, v2 = _extract_topk(s2, kk)
        cand = jnp.concatenate([v1[k:k + 1] + v2 for k in range(kk)], axis=0)
        rc, _ = _extract_topk(cand, kk)
        picked = rc < float(kk)
        e = jnp.where(picked, jnp.exp(cand - (v1[0:1] + v2[0:1])), 0.0)
        z = jnp.sum(e, axis=0, keepdims=True)
        cnt = jnp.zeros_like(r1)
        for k in range(kk):
            c_k = jnp.sum(picked[k * kk:(k + 1) * kk].astype(F32), axis=0, keepdims=True)
            cnt = jnp.where(r1 == float(k), c_k, cnt)
        a_out[hd] = jnp.where(r1 < float(kk), jnp.exp(s1 - v1[0:1]), 0.0) / z
        cnt_out[hd] = cnt
        b_out[hd] = jnp.where(r2 < float(kk), jnp.exp(s2 - v2[0:1]), 0.0)
        r2_out[hd] = r2


def _post_select(mix, x2d, wout, g, b, wq, sk, *, tb, alpha):
    t, d = x2d.shape
    row = lambda w: pl.BlockSpec((tb, w), lambda i: (i, 0))
    sel = pl.BlockSpec((PEER_HEADS, N_KEYS, tb), lambda i: (0, 0, i))
    sel_shape = jax.ShapeDtypeStruct((PEER_HEADS, N_KEYS, t), F32)
    return pl.pallas_call(
        functools.partial(_post_select_kernel, alpha=alpha),
        grid=(t // tb,),
        in_specs=[row(mix.shape[1]), row(d), _full(wout.shape), _full(g.shape), _full(b.shape),
                  _full(wq.shape), _full(sk.shape)],
        out_specs=[row(d), row(d), sel, sel, sel, sel],
        out_shape=[jax.ShapeDtypeStruct((t, d), F32), jax.ShapeDtypeStruct((t, d), BF16),
                   sel_shape, sel_shape, sel_shape, sel_shape],
        compiler_params=_cparams(("parallel",)),
        name="post_select",
    )(mix, x2d, wout, g, b, wq, sk)


def _peer_dense_kernel(h_ref, hb_ref, a_ref, cnt_ref, b_ref, r2_ref, u_ref, v_ref, g_ref, bb_ref,
                       o_ref, acc, *, alpha, rows_per_step):
    e = pl.program_id(1)

    @pl.when(e == 0)
    def _():
        acc[...] = jnp.zeros(acc.shape, F32)

    s = _nt(u_ref[...], hb_ref[...])
    act = 0.5 * s * (1.0 + lax.erf(s * (2.0 ** -0.5)))
    gates = []
    for r in range(rows_per_step):
        i = e * rows_per_step + r
        gate = jnp.zeros((N_KEYS, s.shape[1]), F32)
        for hd in range(PEER_HEADS):
            a_row = a_ref[hd, pl.ds(i, 1), :]
            c_row = cnt_ref[hd, pl.ds(i, 1), :]
            gate = gate + jnp.where(r2_ref[hd] < c_row, b_ref[hd], 0.0) * a_row
        gates.append(gate)
    gate = jnp.concatenate(gates, axis=0) if rows_per_step > 1 else gates[0]
    w = (gate * act).astype(BF16)
    acc[...] += _tn(w, v_ref[...])

    @pl.when(e == pl.num_programs(1) - 1)
    def _():
        o_ref[...] = _layer_norm(alpha * h_ref[...] + acc[...], g_ref[...], bb_ref[...])


def _peer_dense(h, hb, sel_a, sel_cnt, sel_b, sel_r2, u_tab, v_tab, g, b, *, tb, rows_per_step, alpha):
    t, d = h.shape
    n_exp = u_tab.shape[0]
    eb = rows_per_step * N_KEYS
    row = lambda w: pl.BlockSpec((tb, w), lambda i, e: (i, 0))
    sel = pl.BlockSpec((PEER_HEADS, N_KEYS, tb), lambda i, e: (0, 0, i))
    tab = pl.BlockSpec((eb, d), lambda i, e: (e, 0))
    cst = lambda shape: pl.BlockSpec(shape, lambda i, e: (0,) * len(shape))
    return pl.pallas_call(
        functools.partial(_peer_dense_kernel, alpha=alpha, rows_per_step=rows_per_step),
        grid=(t // tb, n_exp // eb),
        in_specs=[row(d), row(d), sel, sel, sel, sel, tab, tab, cst(g.shape), cst(b.shape)],
        out_specs=row(d),
        out_shape=jax.ShapeDtypeStruct((t, d), F32),
        scratch_shapes=[pltpu.VMEM((tb, d), F32)],
        compiler_params=_cparams(("parallel", "arbitrary")),
        name="peer_dense",
    )(h, hb, sel_a, sel_cnt, sel_b, sel_r2, u_tab, v_tab, g, b)


def _rope_tables(pos):
    inv = 1.0 / (ROPE_THETA ** (jnp.arange(0, B_DR, 2, dtype=F32) / B_DR))
    ang = pos.astype(F32)[:, None] * inv[None, :]
    cos, sin = jnp.cos(ang), jnp.sin(ang)
    pad = jnp.zeros((pos.shape[0], LANES - B_DR), F32)
    return (jnp.concatenate([cos, cos, pad], axis=-1), jnp.concatenate([-sin, sin, pad], axis=-1))


def _swap_halves(w):
    half = w.shape[-1] // 2
    return jnp.concatenate([w[..., half:], w[..., :half]], axis=-1)


def _pad_lanes(w):
    return jnp.pad(w, [(0, 0)] * (w.ndim - 1) + [(0, LANES - w.shape[-1])])


def _pick(n, prefs):
    for c in prefs:
        if n % c == 0:
            return c
    return n


def kernel(x_prompt, x_sample, cache_diff_kv, cache_mla, page_table, w_in, diff_lambda, diff_subln_g, rel_bias,
           mla_q_norm_g, mla_w_uq, mla_kv_norm_g, mla_w_uk, mla_w_uv, w_out, ln1_g, ln1_b, peer_w_q,
           peer_sub_keys, peer_u, peer_v, ln2_g, ln2_b):
    batch, seq, d = x_prompt.shape
    sb, dec, _ = x_sample.shape
    depth = w_in.shape[0]
    n_pool, page = cache_mla.shape[1], cache_mla.shape[2]
    n_pages = page_table.shape[1]
    past = n_pages * page
    q_lora = mla_q_norm_g.shape[1]
    kv_lora = mla_kv_norm_g.shape[1]
    mla_row = kv_lora + B_DR
    alpha = (2 * depth) ** 0.25
    aw = A_WIDTH

    pos_p = jnp.arange(seq, dtype=jnp.int32)
    pos_s = past + jnp.arange(dec, dtype=jnp.int32)
    cos_p, sin_p = _rope_tables(jnp.tile(pos_p, batch))
    cos_s, sin_s = _rope_tables(jnp.tile(pos_s, sb))

    blk = _pick(seq, (256, 128))
    far_rel = 2 * blk
    assert REL_EXACT * (REL_MAX_DIST / REL_EXACT) ** ((REL_BUCKETS - 1 - REL_EXACT) / (REL_BUCKETS - REL_EXACT)) < blk
    rel_fn = lambda rel: jnp.transpose(rel_bias[_t5_bucket(rel)].astype(F32), (2, 0, 1))
    r = jnp.arange(blk, dtype=jnp.int32)
    rel_d = r[:, None] - r[None, :]
    diag = jnp.where(rel_d[None] >= 0, rel_fn(rel_d), NEG)
    off1 = rel_fn(rel_d + blk)
    far = rel_fn(jnp.full((blk, blk), far_rel, jnp.int32))
    zeros = jnp.zeros((1, blk, blk), F32)
    bias_p = jnp.stack([jnp.concatenate([diag, jnp.where(rel_d[None] >= 0, 0.0, NEG)], axis=0),
                        jnp.concatenate([off1, zeros], axis=0),
                        jnp.concatenate([far, zeros], axis=0)])

    qpos = pos_s
    rel_s = qpos[:, None] - jnp.arange(past, dtype=jnp.int32)[None, :]
    bias_s = rel_fn(rel_s)
    bias_s = jnp.broadcast_to(bias_s[:, None], (A_HEADS, 2, dec, past)).reshape(2 * A_HEADS * dec, n_pages, page)
    bias_s = jnp.transpose(bias_s, (1, 0, 2))
    rd = jnp.arange(dec, dtype=jnp.int32)
    rel_n = rd[:, None] - rd[None, :]
    bias_n = jnp.where(rel_n[None] >= 0, rel_fn(rel_n), NEG)
    bias_n = jnp.broadcast_to(bias_n[:, None], (A_HEADS, 2, dec, dec)).reshape(2 * A_HEADS * dec, dec)
    mask_n = jnp.tile(jnp.where(rel_n >= 0, 0.0, NEG).astype(F32), (B_HEADS, 1))

    cache_kv2 = cache_diff_kv.reshape(depth * n_pool, page, 2 * aw)
    cache_mla2 = cache_mla.reshape(depth * n_pool, page, mla_row)

    tm = _pick(batch * seq, (512, 256, 128))
    tm_s = _pick(sb * dec, (512, 256, 128))
    tb = _pick(batch * seq, (256, 128))
    tb_s = _pick(sb * dec, (256, 128))
    n_per_step = _pick(n_pages, (8, 4, 2, 1))

    hp = x_prompt.reshape(batch * seq, d)
    hs = x_sample.reshape(sb * dec, d)
    kv_p, mla_p, kv_s, mla_s = [], [], [], []
    for l in range(depth):
        lam_init = 0.8 - 0.6 * math.exp(-0.3 * l)
        w = w_in[l]
        o = 3 * aw + q_lora + kv_lora
        kr_w = w[:, o:]
        win = jnp.concatenate([w[:, :o], _pad_lanes(kr_w), _pad_lanes(_swap_halves(kr_w))], axis=-1).astype(BF16)
        uq = mla_w_uq[l]
        uq_r = uq[..., B_DN:]
        wuq = jnp.concatenate([uq[..., :B_DN].reshape(q_lora, -1), _pad_lanes(uq_r).reshape(q_lora, -1),
                               _pad_lanes(_swap_halves(uq_r)).reshape(q_lora, -1)], axis=-1).astype(BF16)
        wuk = jnp.transpose(mla_w_uk[l], (1, 2, 0)).astype(BF16)
        wuv = jnp.transpose(mla_w_uv[l], (1, 0, 2)).astype(BF16)
        qg = mla_q_norm_g[l].reshape(1, -1)
        kvg = mla_kv_norm_g[l].reshape(1, -1)
        subg = diff_subln_g[l].reshape(1, -1)
        lam_p = diff_lambda[l]
        wout = w_out[l].astype(BF16)
        wq = peer_w_q[l].astype(BF16)
        sk = peer_sub_keys[l].astype(BF16)
        u_tab = peer_u[l].astype(BF16)
        v_tab = peer_v[l].astype(BF16)
        g1, b1 = ln1_g[l].reshape(1, -1), ln1_b[l].reshape(1, -1)
        g2, b2 = ln2_g[l].reshape(1, -1), ln2_b[l].reshape(1, -1)

        def post(h2d, mix, tb_):
            h1, h1b, sa, sc, sbb, sr = _post_select(mix, h2d, wout, g1, b1, wq, sk, tb=tb_, alpha=alpha)
            return _peer_dense(h1, h1b, sa, sc, sbb, sr, u_tab, v_tab, g2, b2, tb=tb_, rows_per_step=2,
                               alpha=alpha)

        kv_rows, mla_rows, qd, kd, vd, qm, km = _project(hp, cos_p, sin_p, win, qg, wuq, kvg, wuk, tm=tm,
                                                         q_lora=q_lora, kv_lora=kv_lora)
        mix = _prompt_attn(qd, qm, kd, vd, km, bias_p, lam_p, subg, wuv, batch=batch, seq=seq, blk=blk,
                           lam_init=lam_init, kv_lora=kv_lora)
        kv_p.append(kv_rows.reshape(batch, seq, 2, A_HEADS, A_DV))
        mla_p.append(mla_rows.reshape(batch, seq, mla_row))
        hp = post(hp, mix, tb)

        kv_rows, mla_rows, qd, kd, vd, qm, km = _project(hs, cos_s, sin_s, win, qg, wuq, kvg, wuk, tm=tm_s,
                                                         q_lora=q_lora, kv_lora=kv_lora)
        r3 = lambda a: a.reshape(sb, dec, a.shape[-1])
        mix = _sample_attn(page_table, r3(qd), r3(qm), r3(kd), r3(vd), r3(km), cache_kv2, cache_mla2,
                           bias_s, bias_n, mask_n, lam_p, subg, wuv, layer_off=l * n_pool, lam_init=lam_init,
                           kv_lora=kv_lora, n_per_step=n_per_step)
        kv_s.append(kv_rows.reshape(sb, dec, 2, A_HEADS, A_DV))
        mla_s.append(mla_rows.reshape(sb, dec, mla_row))
        hs = post(hs, mix.reshape(sb * dec, -1), tb_s)

    return (hp.reshape(batch, seq, d), hs.reshape(sb, dec, d), jnp.stack(kv_p), jnp.stack(mla_p),
            jnp.stack(kv_s), jnp.stack(mla_s))
```

```python
import functools
import math

import numpy as np
import jax
import jax.numpy as jnp
from jax import lax
from jax.experimental import pallas as pl
from jax.experimental.pallas import tpu as pltpu

F32 = jnp.float32
BF16 = jnp.bfloat16

A_HEADS = 4
A_HALF = 64
A_DV = 128
A_WIDTH = A_HEADS * A_DV
B_HEADS = 4
B_DV = 128
B_DN = 128
B_DR = 64
ROPE_THETA = 10000.0
REL_BUCKETS = 32
REL_MAX_DIST = 128
REL_EXACT = REL_BUCKETS // 2
PEER_HEADS = 8
PEER_DK = 256
N_KEYS = 128
PEER_TOPK = 16
NEG = -1e30
LOG2E = math.log2(math.e)

SUBLANES = 8
LANES = 128
MLA_PAD = 384
VMEM_LIMIT = 48 * 1024 * 1024


def _cparams(sem):
    return pltpu.CompilerParams(dimension_semantics=sem, vmem_limit_bytes=VMEM_LIMIT)


def _full(shape):
    n = len(shape)
    return pl.BlockSpec(shape, lambda *_: (0,) * n)


def _t5_bucket(rel):
    n = jnp.maximum(rel, 0)
    large = REL_EXACT + (jnp.log(jnp.maximum(n, 1).astype(F32) / REL_EXACT)
                         / math.log(REL_MAX_DIST / REL_EXACT) * (REL_BUCKETS - REL_EXACT)).astype(jnp.int32)
    large = jnp.minimum(large, REL_BUCKETS - 1)
    return jnp.where(n < REL_EXACT, n, large)


def _nt(a, b):
    return lax.dot_general(a, b, (((1,), (1,)), ((), ())), preferred_element_type=F32)


def _tn(a, b):
    return lax.dot_general(a, b, (((0,), (0,)), ((), ())), preferred_element_type=F32)


def _dot(a, b):
    return jnp.dot(a, b, preferred_element_type=F32)


def _project_kernel(x_ref, win_ref, qg_ref, wuq_ref, kvg_ref, wuk_ref, cos_ref, sin_ref,
                    kv_out, mlat_out, qd_out, kd_out, vd_out, vdt_out, qm_out, km_out, ct_out,
                    *, a_scale, mla_scale, q_lora, kv_lora):
    tm = x_ref.shape[0]
    x = x_ref[...].astype(BF16)
    z = _dot(x, win_ref[...])
    aw = A_WIDTH
    for j in range(2 * A_HEADS):
        kv_out[pl.ds(j, tm, stride=2 * A_HEADS), :] = z[:, aw + j * A_DV:aw + (j + 1) * A_DV]
    qd_out[...] = (z[:, :aw] * a_scale).astype(BF16)
    kd_out[...] = z[:, aw:2 * aw].astype(BF16)
    v = z[:, 2 * aw:3 * aw]
    vd_out[...] = v.astype(BF16)
    vdt_out[...] = v.T.astype(BF16)
    o = 3 * aw
    cq = z[:, o:o + q_lora]
    o += q_lora
    ckv = z[:, o:o + kv_lora]
    o += kv_lora
    kr = z[:, o:o + LANES]
    krs = z[:, o + LANES:o + 2 * LANES]
    cos = cos_ref[...]
    sin = sin_ref[...]

    cqn = cq * lax.rsqrt(jnp.mean(cq * cq, axis=-1, keepdims=True) + 1e-6) * qg_ref[...]
    q = _dot(cqn.astype(BF16), wuq_ref[...])
    ckvn = ckv * lax.rsqrt(jnp.mean(ckv * ckv, axis=-1, keepdims=True) + 1e-6) * kvg_ref[...]
    krr = kr * cos + krs * sin

    km_out[:, :kv_lora] = ckvn.astype(BF16)
    km_out[:, kv_lora:] = krr.astype(BF16)
    mla_t = jnp.concatenate([ckvn, krr], axis=1).T
    mlat_out[0] = mla_t[:kv_lora + B_DR]
    ct_out[...] = mla_t[:kv_lora].astype(BF16)

    nope_w = B_HEADS * B_DN
    rope_w = B_HEADS * LANES
    for h in range(B_HEADS):
        ql = _dot(q[:, h * B_DN:(h + 1) * B_DN].astype(BF16), wuk_ref[h])
        qr = (q[:, nope_w + h * LANES:nope_w + (h + 1) * LANES] * cos
              + q[:, nope_w + rope_w + h * LANES:nope_w + rope_w + (h + 1) * LANES] * sin)
        base = h * MLA_PAD
        qm_out[:, base:base + kv_lora] = (ql * mla_scale).astype(BF16)
        qm_out[:, base + kv_lora:base + MLA_PAD] = (qr * mla_scale).astype(BF16)


def _project(x2d, cos, sin, win, qg, wuq, kvg, wuk, *, groups, tm, q_lora, kv_lora):
    t, d = x2d.shape
    aw = A_WIDTH
    mla_row = kv_lora + B_DR
    tiles_per_group = t // groups // tm
    kern = functools.partial(_project_kernel, a_scale=A_HALF ** -0.5 * LOG2E,
                             mla_scale=(B_DN + B_DR) ** -0.5 * LOG2E, q_lora=q_lora, kv_lora=kv_lora)
    row = lambda w: pl.BlockSpec((tm, w), lambda i: (i, 0))
    col = lambda w: pl.BlockSpec((w, tm), lambda i: (0, i))
    return pl.pallas_call(
        kern,
        grid=(t // tm,),
        in_specs=[row(d), _full(win.shape), _full(qg.shape), _full(wuq.shape), _full(kvg.shape),
                  _full(wuk.shape), row(LANES), row(LANES)],
        out_specs=[pl.BlockSpec((tm * 2 * A_HEADS, A_DV), lambda i: (i, 0)),
                   pl.BlockSpec((1, mla_row, tm), lambda i: (i // tiles_per_group, 0, i % tiles_per_group)),
                   row(aw), row(aw), row(aw), col(aw), row(B_HEADS * MLA_PAD), row(MLA_PAD), col(kv_lora)],
        out_shape=[jax.ShapeDtypeStruct((t * 2 * A_HEADS, A_DV), F32),
                   jax.ShapeDtypeStruct((groups, mla_row, t // groups), F32),
                   jax.ShapeDtypeStruct((t, aw), BF16), jax.ShapeDtypeStruct((t, aw), BF16),
                   jax.ShapeDtypeStruct((t, aw), BF16), jax.ShapeDtypeStruct((aw, t), BF16),
                   jax.ShapeDtypeStruct((t, B_HEADS * MLA_PAD), BF16),
                   jax.ShapeDtypeStruct((t, MLA_PAD), BF16), jax.ShapeDtypeStruct((kv_lora, t), BF16)],
        compiler_params=_cparams(("parallel",)),
        name="project",
    )(x2d, win, qg, wuq, kvg, wuk, cos, sin)


def _lambda_value(lam_ref, lam_init):
    lp = lam_ref[...]
    l01 = jnp.sum(lp[0:1] * lp[1:2], axis=-1, keepdims=True)
    l23 = jnp.sum(lp[2:3] * lp[3:4], axis=-1, keepdims=True)
    return jnp.exp(l01) - jnp.exp(l23) + lam_init


def _update_t(s_t, pv, m_ref, l_ref, acc_ref):
    m_prev = m_ref[...]
    m_new = jnp.maximum(m_prev, jnp.max(s_t, axis=0, keepdims=True))
    alpha = jnp.exp2(m_prev - m_new)
    e = jnp.exp2(s_t - m_new)
    l_ref[...] = alpha * l_ref[...] + jnp.sum(e, axis=0, keepdims=True)
    acc_ref[...] = alpha * acc_ref[...] + pv(e.astype(BF16))
    m_ref[...] = m_new


def _prompt_attn_kernel(qi_ref, kj_ref, qd_ref, qm_ref, kd_ref, vdt_ref, km_ref, ct_ref, bias_ref, lam_ref,
                        subg_ref, wuvt_ref, o_ref, m_d, l_d, acc_d, m_m, l_m, acc_m, *, lam_init):
    p = pl.program_id(1)
    qi = qi_ref[p]
    kj = kj_ref[p]

    @pl.when(kj == 0)
    def _():
        m_d[...] = jnp.full(m_d.shape, -jnp.inf, F32)
        l_d[...] = jnp.zeros(l_d.shape, F32)
        acc_d[...] = jnp.zeros(acc_d.shape, F32)
        m_m[...] = jnp.full(m_m.shape, -jnp.inf, F32)
        l_m[...] = jnp.zeros(l_m.shape, F32)
        acc_m[...] = jnp.zeros(acc_m.shape, F32)

    bq = qd_ref.shape[0]
    lane = lax.broadcasted_iota(jnp.int32, (bq, A_DV), 1)
    parts, biases = [], []
    for h in range(A_HEADS):
        hs = slice(h * A_DV, (h + 1) * A_DV)
        qh = qd_ref[:, hs]
        zero = jnp.zeros_like(qh)
        q2 = jnp.concatenate([jnp.where(lane < A_HALF, qh, zero), jnp.where(lane >= A_HALF, qh, zero)], axis=0)
        parts.append(_nt(kd_ref[:, hs], q2))
        biases += [bias_ref[0, h]] * 2
    s_t = jnp.concatenate(parts, axis=1) + jnp.concatenate(biases, axis=1)

    def pv_d(e):
        return jnp.concatenate([_dot(vdt_ref[h * A_DV:(h + 1) * A_DV, :], e[:, 2 * h * bq:2 * (h + 1) * bq])
                                for h in range(A_HEADS)], axis=1)

    _update_t(s_t, pv_d, m_d, l_d, acc_d)

    q_rows = jnp.concatenate([qm_ref[:, h * MLA_PAD:(h + 1) * MLA_PAD] for h in range(B_HEADS)], axis=0)
    s_t = _nt(km_ref[...], q_rows) + jnp.concatenate([bias_ref[0, A_HEADS]] * B_HEADS, axis=1)
    _update_t(s_t, lambda e: _dot(ct_ref[...], e), m_m, l_m, acc_m)

    @pl.when(kj == qi)
    def _():
        lam = _lambda_value(lam_ref, lam_init)
        o_d = acc_d[...] / l_d[...]
        for h in range(A_HEADS):
            o_a = o_d[:, 2 * h * bq:(2 * h + 1) * bq]
            o_b = o_d[:, (2 * h + 1) * bq:(2 * h + 2) * bq]
            w = o_a - lam * o_b
            w = w * lax.rsqrt(jnp.mean(w * w, axis=0, keepdims=True) + 1e-5) * subg_ref[...] * (1.0 - lam_init)
            o_ref[h * A_DV:(h + 1) * A_DV, :] = w.astype(o_ref.dtype)
        o_m = (acc_m[...] / l_m[...]).astype(BF16)
        for h in range(B_HEADS):
            o = _dot(wuvt_ref[h], o_m[:, h * bq:(h + 1) * bq])
            o_ref[A_WIDTH + h * B_DV:A_WIDTH + (h + 1) * B_DV, :] = o.astype(o_ref.dtype)


def _prompt_attn(qd, qm, kd, vdt, km, ct, bias_t, lam_p, subg_col, wuvt, *, batch, seq, blk, lam_init, kv_lora):
    nq = seq // blk
    pairs = [(i, j) for i in range(nq) for j in range(i + 1)]
    qi_arr = jnp.asarray(np.array([p[0] for p in pairs], np.int32))
    kj_arr = jnp.asarray(np.array([p[1] for p in pairs], np.int32))
    mix_w = A_WIDTH + B_HEADS * B_DV
    qmap = lambda b, p, qi, kj: (b * nq + qi[p], 0)
    kmap = lambda b, p, qi, kj: (b * nq + kj[p], 0)
    kmap_t = lambda b, p, qi, kj: (0, b * nq + kj[p])
    bmap = lambda b, p, qi, kj: (jnp.minimum(qi[p] - kj[p], 2), 0, 0, 0)
    cst = lambda shape: pl.BlockSpec(shape, lambda b, p, qi, kj: (0,) * len(shape))
    grid_spec = pltpu.PrefetchScalarGridSpec(
        num_scalar_prefetch=2,
        grid=(batch, len(pairs)),
        in_specs=[pl.BlockSpec((blk, A_WIDTH), qmap), pl.BlockSpec((blk, B_HEADS * MLA_PAD), qmap),
                  pl.BlockSpec((blk, A_WIDTH), kmap), pl.BlockSpec((A_WIDTH, blk), kmap_t),
                  pl.BlockSpec((blk, MLA_PAD), kmap), pl.BlockSpec((kv_lora, blk), kmap_t),
                  pl.BlockSpec((1, A_HEADS + 1, blk, blk), bmap),
                  cst(lam_p.shape), cst(subg_col.shape), cst(wuvt.shape)],
        out_specs=pl.BlockSpec((mix_w, blk), lambda b, p, qi, kj: (0, b * nq + qi[p])),
        scratch_shapes=[pltpu.VMEM((1, 2 * A_HEADS * blk), F32), pltpu.VMEM((1, 2 * A_HEADS * blk), F32),
                        pltpu.VMEM((A_DV, 2 * A_HEADS * blk), F32),
                        pltpu.VMEM((1, B_HEADS * blk), F32), pltpu.VMEM((1, B_HEADS * blk), F32),
                        pltpu.VMEM((kv_lora, B_HEADS * blk), F32)],
    )
    return pl.pallas_call(
        functools.partial(_prompt_attn_kernel, lam_init=lam_init),
        grid_spec=grid_spec,
        out_shape=jax.ShapeDtypeStruct((mix_w, batch * seq), BF16),
        compiler_params=_cparams(("parallel", "arbitrary")),
        name="prompt_attn",
    )(qi_arr, kj_arr, qd, qm, kd, vdt, km, ct, bias_t, lam_p, subg_col, wuvt)


def _update(s, pv, m_ref, l_ref, acc_ref):
    m_prev = m_ref[...]
    m_new = jnp.maximum(m_prev, jnp.max(s, axis=-1, keepdims=True))
    alpha = jnp.exp2(m_prev - m_new)
    e = jnp.exp2(s - m_new)
    l_ref[...] = alpha * l_ref[...] + jnp.sum(e, axis=-1, keepdims=True)
    acc_ref[...] = alpha * acc_ref[...] + pv(e.astype(BF16))
    m_ref[...] = m_new


def _sample_attn_kernel(pt_ref, qd_ref, qm_ref, kdn_ref, vdn_ref, kmn_ref, bias_ref, biasn_ref, maskn_ref,
                        lam_ref, subg_ref, wuv_ref, *rest, lam_init, kv_lora, n_per_step):
    kv_refs = rest[:n_per_step]
    mla_refs = rest[n_per_step:2 * n_per_step]
    o_ref, qbd, qmm, m_d, l_d, acc_d, m_m, l_m, acc_m = rest[2 * n_per_step:]
    p = pl.program_id(1)
    dec = qd_ref.shape[1]
    n_maps = 2 * A_HEADS
    mla_row = kv_lora + B_DR
    page = mla_refs[0].shape[2]

    @pl.when(p == 0)
    def _():
        q8 = qd_ref[0].astype(F32)
        q_rep = jnp.concatenate([q8] * n_maps, axis=0)
        row = lax.broadcasted_iota(jnp.int32, q_rep.shape, 0)
        lane = lax.broadcasted_iota(jnp.int32, q_rep.shape, 1)
        qbd[...] = jnp.where(lane // A_HALF == row // dec, q_rep, 0.0).astype(BF16)
        qm8 = qm_ref[0].astype(F32)
        qmm[...] = jnp.concatenate([qm8[:, h * MLA_PAD:h * MLA_PAD + mla_row] for h in range(B_HEADS)],
                                   axis=0).astype(BF16)
        m_d[...] = jnp.full(m_d.shape, -jnp.inf, F32)
        l_d[...] = jnp.zeros(l_d.shape, F32)
        acc_d[...] = jnp.zeros(acc_d.shape, F32)
        m_m[...] = jnp.full(m_m.shape, -jnp.inf, F32)
        l_m[...] = jnp.zeros(l_m.shape, F32)
        acc_m[...] = jnp.zeros(acc_m.shape, F32)

    q_d = qbd[...]
    q_m = qmm[...]
    heads = lambda ref, off: jnp.concatenate(
        [ref[pl.ds(off + h, page, stride=2 * A_HEADS), :].astype(BF16) for h in range(A_HEADS)], axis=1)
    k_all = jnp.concatenate([heads(r, 0) for r in kv_refs], axis=0)
    v_all = jnp.concatenate([heads(r, A_HEADS) for r in kv_refs], axis=0)
    r_all = jnp.concatenate([r[0].astype(BF16) for r in mla_refs], axis=1)

    _update(_nt(q_d, k_all) + bias_ref[0], lambda e: _dot(e, v_all), m_d, l_d, acc_d)
    _update(_dot(q_m, r_all), lambda e: _nt(e, r_all[:kv_lora]), m_m, l_m, acc_m)

    @pl.when(p == pl.num_programs(1) - 1)
    def _():
        vdn = vdn_ref[0]
        _update(_nt(q_d, kdn_ref[0]) + biasn_ref[...], lambda e: _dot(e, vdn), m_d, l_d, acc_d)
        kmn = kmn_ref[0]
        _update(_nt(q_m, kmn[:, :mla_row]) + maskn_ref[...], lambda e: _dot(e, kmn[:, :kv_lora]), m_m, l_m, acc_m)

        lam = _lambda_value(lam_ref, lam_init)
        o_all = acc_d[...] / l_d[...]
        for h in range(A_HEADS):
            ra = (2 * h) * dec
            rb = (2 * h + 1) * dec
            o_a = o_all[ra:ra + dec, h * A_DV:(h + 1) * A_DV]
            o_b = o_all[rb:rb + dec, h * A_DV:(h + 1) * A_DV]
            w = o_a - lam * o_b
            w = w * lax.rsqrt(jnp.mean(w * w, axis=-1, keepdims=True) + 1e-5) * subg_ref[...] * (1.0 - lam_init)
            o_ref[0, :, h * A_DV:(h + 1) * A_DV] = w.astype(o_ref.dtype)
        ol_all = acc_m[...] / l_m[...]
        for h in range(B_HEADS):
            ol = ol_all[h * dec:(h + 1) * dec]
            o = _dot(ol.astype(BF16), wuv_ref[h])
            o_ref[0, :, A_WIDTH + h * B_DV:A_WIDTH + (h + 1) * B_DV] = o.astype(o_ref.dtype)


def _sample_attn(page_table, qd, qm, kdn, vdn, kmn, cache_kv, cache_mla_t, bias, biasn, maskn, lam_p, subg, wuv,
                 *, layer_off, lam_init, kv_lora, n_per_step):
    sb, dec, _ = qd.shape
    n_pages = page_table.shape[1]
    mla_row, page = cache_mla_t.shape[1], cache_mla_t.shape[2]
    mix_w = A_WIDTH + B_HEADS * B_DV
    steps = n_pages // n_per_step
    rows_d = 2 * A_HEADS * dec
    rows_m = B_HEADS * dec
    pt_flat = page_table.reshape(-1)

    seq3 = lambda w: pl.BlockSpec((1, dec, w), lambda s, p, pt: (s, 0, 0))
    cst = lambda shape: pl.BlockSpec(shape, lambda s, p, pt: (0,) * len(shape))

    def page_id(j):
        return lambda s, p, pt: pt[s * n_pages + p * n_per_step + j] + layer_off

    in_specs = [seq3(A_WIDTH), seq3(B_HEADS * MLA_PAD), seq3(A_WIDTH), seq3(A_WIDTH), seq3(MLA_PAD),
                pl.BlockSpec((1, rows_d, n_per_step * page),
                             lambda s, p, pt: (jnp.where(p == steps - 1, 1, 0), 0, 0)),
                cst(biasn.shape), cst(maskn.shape), cst(lam_p.shape), cst(subg.shape), cst(wuv.shape)]
    for j in range(n_per_step):
        in_specs.append(pl.BlockSpec((page * 2 * A_HEADS, A_DV), lambda s, p, pt, f=page_id(j): (f(s, p, pt), 0)))
    for j in range(n_per_step):
        in_specs.append(pl.BlockSpec((1, mla_row, page), lambda s, p, pt, f=page_id(j): (f(s, p, pt), 0, 0)))
    kern = functools.partial(_sample_attn_kernel, lam_init=lam_init, kv_lora=kv_lora, n_per_step=n_per_step)
    grid_spec = pltpu.PrefetchScalarGridSpec(
        num_scalar_prefetch=1,
        grid=(sb, steps),
        in_specs=in_specs,
        out_specs=pl.BlockSpec((1, dec, mix_w), lambda s, p, pt: (s, 0, 0)),
        scratch_shapes=[pltpu.VMEM((rows_d, A_WIDTH), BF16), pltpu.VMEM((rows_m, mla_row), BF16),
                        pltpu.VMEM((rows_d, 1), F32), pltpu.VMEM((rows_d, 1), F32),
                        pltpu.VMEM((rows_d, A_WIDTH), F32),
                        pltpu.VMEM((rows_m, 1), F32), pltpu.VMEM((rows_m, 1), F32),
                        pltpu.VMEM((rows_m, kv_lora), F32)],
    )
    return pl.pallas_call(
        kern,
        grid_spec=grid_spec,
        out_shape=jax.ShapeDtypeStruct((sb, dec, mix_w), BF16),
        compiler_params=_cparams(("parallel", "arbitrary")),
        name="sample_attn",
    )(pt_flat, qd, qm, kdn, vdn, kmn, bias, biasn, maskn, lam_p, subg, wuv,
      *([cache_kv] * n_per_step), *([cache_mla_t] * n_per_step))


def _layer_norm(x, g, b):
    xc = x - jnp.mean(x, axis=-1, keepdims=True)
    var = jnp.mean(xc * xc, axis=-1, keepdims=True)
    return xc * lax.rsqrt(var + 1e-5) * g + b


def _extract_topk(s, k_top, want_rank):
    n, t = s.shape
    iota = lax.broadcasted_iota(jnp.int32, (n, t), 0).astype(F32)
    iota_k = lax.broadcasted_iota(jnp.int32, (k_top, t), 0)
    rank = jnp.full((n, t), float(n), F32) if want_rank else None
    vals = jnp.zeros((k_top, t), F32)
    picks = []
    for k in range(k_top):
        m = jnp.max(s, axis=0, keepdims=True)
        idx = jnp.min(jnp.where(s == m, iota, float(n)), axis=0, keepdims=True)
        sel = iota == idx
        if want_rank:
            rank = jnp.where(sel, float(k), rank)
        s = jnp.where(sel, -jnp.inf, s)
        vals = jnp.where(iota_k == k, m, vals)
        picks.append(idx)
    return rank, vals, picks, s


def _post_select_kernel(mix_ref, x_ref, wout_ref, g_ref, b_ref, wq_ref, sk_ref,
                        h_out, hb_out, a_out, cnt_out, b_out, r2_out, *, alpha, mix_transposed):
    a = _tn(mix_ref[...], wout_ref[...]) if mix_transposed else _dot(mix_ref[...], wout_ref[...])
    h = _layer_norm(alpha * x_ref[...] + a, g_ref[...], b_ref[...])
    h_out[...] = h
    hb = h.astype(BF16)
    hb_out[...] = hb
    q = _dot(hb, wq_ref[...])
    half = PEER_DK // 2
    kk = PEER_TOPK
    sub = kk // 2
    iota_n = lax.broadcasted_iota(jnp.int32, (N_KEYS, q.shape[0]), 0).astype(F32)
    for hd in range(PEER_HEADS):
        q1 = q[:, hd * PEER_DK:hd * PEER_DK + half].astype(BF16)
        q2 = q[:, hd * PEER_DK + half:(hd + 1) * PEER_DK].astype(BF16)
        s1 = _nt(sk_ref[0], q1)
        s2 = _nt(sk_ref[1], q2)
        _, v1, picks1, s1_left = _extract_topk(s1, kk, False)
        r2, v2, _, _ = _extract_topk(s2, kk, True)
        cand = jnp.concatenate([v1[0:1] + v2] + [v1[k:k + 1] + v2[:sub] for k in range(1, kk)], axis=0)
        _, _, _, cand_left = _extract_topk(cand, kk, False)
        picked = cand_left == -jnp.inf
        e = jnp.where(picked, jnp.exp(cand - (v1[0:1] + v2[0:1])), 0.0)
        z = jnp.sum(e, axis=0, keepdims=True)
        pf = picked.astype(F32)
        cnt = jnp.zeros_like(s1)
        for k in range(kk):
            rows = pf[0:kk] if k == 0 else pf[kk + (k - 1) * sub:kk + k * sub]
            c_k = jnp.sum(rows, axis=0, keepdims=True)
            cnt = jnp.where(iota_n == picks1[k], c_k, cnt)
        a_out[hd] = jnp.where(s1_left == -jnp.inf, jnp.exp(s1 - v1[0:1]), 0.0) / z
        cnt_out[hd] = cnt
        b_out[hd] = jnp.where(r2 < float(kk), jnp.exp(s2 - v2[0:1]), 0.0)
        r2_out[hd] = r2


def _post_select(mix, x2d, wout, g, b, wq, sk, *, tb, alpha, mix_transposed):
    t, d = x2d.shape
    row = lambda w: pl.BlockSpec((tb, w), lambda i: (i, 0))
    mix_spec = pl.BlockSpec((mix.shape[0], tb), lambda i: (0, i)) if mix_transposed else row(mix.shape[1])
    sel = pl.BlockSpec((PEER_HEADS, N_KEYS, tb), lambda i: (0, 0, i))
    sel_shape = jax.ShapeDtypeStruct((PEER_HEADS, N_KEYS, t), F32)
    return pl.pallas_call(
        functools.partial(_post_select_kernel, alpha=alpha, mix_transposed=mix_transposed),
        grid=(t // tb,),
        in_specs=[mix_spec, row(d), _full(wout.shape), _full(g.shape), _full(b.shape),
                  _full(wq.shape), _full(sk.shape)],
        out_specs=[row(d), row(d), sel, sel, sel, sel],
        out_shape=[jax.ShapeDtypeStruct((t, d), F32), jax.ShapeDtypeStruct((t, d), BF16),
                   sel_shape, sel_shape, sel_shape, sel_shape],
        compiler_params=_cparams(("parallel",)),
        name="post_select",
    )(mix, x2d, wout, g, b, wq, sk)


def _peer_dense_kernel(h_ref, hb_ref, a_ref, cnt_ref, b_ref, r2_ref, u_ref, v_ref, g_ref, bb_ref,
                       o_ref, acc, s_scr, w_scr, *, alpha, rows_per_step):
    e = pl.program_id(1)

    @pl.when(e == 0)
    def _():
        acc[...] = jnp.zeros(acc.shape, F32)

    s_scr[...] = _nt(u_ref[...], hb_ref[...])

    def row_body(r, carry):
        i = e * rows_per_step + r
        rows = pl.ds(pl.multiple_of(r * N_KEYS, N_KEYS), N_KEYS)
        s = s_scr[rows, :]
        act = 0.5 * s * (1.0 + lax.erf(s * (2.0 ** -0.5)))
        gate = jnp.zeros_like(s)
        for hd in range(PEER_HEADS):
            a_row = a_ref[hd, pl.ds(i, 1), :]
            c_row = cnt_ref[hd, pl.ds(i, 1), :]
            gate = gate + jnp.where(r2_ref[hd] < c_row, b_ref[hd], 0.0) * a_row
        w_scr[rows, :] = (gate * act).astype(BF16)
        return carry

    lax.fori_loop(0, rows_per_step, row_body, 0)
    acc[...] += _tn(w_scr[...], v_ref[...])

    @pl.when(e == pl.num_programs(1) - 1)
    def _():
        o_ref[...] = _layer_norm(alpha * h_ref[...] + acc[...], g_ref[...], bb_ref[...])


def _peer_dense(h, hb, sel_a, sel_cnt, sel_b, sel_r2, u_tab, v_tab, g, b, *, tb, rows_per_step, alpha):
    t, d = h.shape
    n_exp = u_tab.shape[0]
    eb = rows_per_step * N_KEYS
    row = lambda w: pl.BlockSpec((tb, w), lambda i, e: (i, 0))
    sel = pl.BlockSpec((PEER_HEADS, N_KEYS, tb), lambda i, e: (0, 0, i))
    tab = pl.BlockSpec((eb, d), lambda i, e: (e, 0))
    cst = lambda shape: pl.BlockSpec(shape, lambda i, e: (0,) * len(shape))
    return pl.pallas_call(
        functools.partial(_peer_dense_kernel, alpha=alpha, rows_per_step=rows_per_step),
        grid=(t // tb, n_exp // eb),
        in_specs=[row(d), row(d), sel, sel, sel, sel, tab, tab, cst(g.shape), cst(b.shape)],
        out_specs=row(d),
        out_shape=jax.ShapeDtypeStruct((t, d), F32),
        scratch_shapes=[pltpu.VMEM((tb, d), F32), pltpu.VMEM((eb, tb), F32), pltpu.VMEM((eb, tb), BF16)],
        compiler_params=_cparams(("parallel", "arbitrary")),
        name="peer_dense",
    )(h, hb, sel_a, sel_cnt, sel_b, sel_r2, u_tab, v_tab, g, b)


def _rope_tables(pos):
    inv = 1.0 / (ROPE_THETA ** (jnp.arange(0, B_DR, 2, dtype=F32) / B_DR))
    ang = pos.astype(F32)[:, None] * inv[None, :]
    cos, sin = jnp.cos(ang), jnp.sin(ang)
    pad = jnp.zeros((pos.shape[0], LANES - B_DR), F32)
    return (jnp.concatenate([cos, cos, pad], axis=-1), jnp.concatenate([-sin, sin, pad], axis=-1))


def _swap_halves(w):
    half = w.shape[-1] // 2
    return jnp.concatenate([w[..., half:], w[..., :half]], axis=-1)


def _pad_lanes(w):
    return jnp.pad(w, [(0, 0)] * (w.ndim - 1) + [(0, LANES - w.shape[-1])])


def _pick(n, prefs):
    for c in prefs:
        if n % c == 0:
            return c
    return n


def _rel_bias_values(rel_bias, rel):
    onehot = _t5_bucket(rel)[..., None, None] == jnp.arange(REL_BUCKETS)[:, None]
    vals = jnp.sum(jnp.where(onehot, rel_bias.astype(F32), 0.0), axis=-2)
    return jnp.moveaxis(vals, -1, 0) * LOG2E


def kernel(x_prompt, x_sample, cache_diff_kv, cache_mla, page_table, w_in, diff_lambda, diff_subln_g, rel_bias,
           mla_q_norm_g, mla_w_uq, mla_kv_norm_g, mla_w_uk, mla_w_uv, w_out, ln1_g, ln1_b, peer_w_q,
           peer_sub_keys, peer_u, peer_v, ln2_g, ln2_b):
    batch, seq, d = x_prompt.shape
    sb, dec, _ = x_sample.shape
    depth = w_in.shape[0]
    n_pool, page = cache_mla.shape[1], cache_mla.shape[2]
    n_pages = page_table.shape[1]
    past = n_pages * page
    q_lora = mla_q_norm_g.shape[1]
    kv_lora = mla_kv_norm_g.shape[1]
    mla_row = kv_lora + B_DR
    alpha = (2 * depth) ** 0.25
    aw = A_WIDTH

    pos_p = jnp.arange(seq, dtype=jnp.int32)
    pos_s = past + jnp.arange(dec, dtype=jnp.int32)
    cos_p, sin_p = _rope_tables(jnp.tile(pos_p, batch))
    cos_s, sin_s = _rope_tables(jnp.tile(pos_s, sb))

    rel_sat = REL_EXACT * (REL_MAX_DIST / REL_EXACT) ** ((REL_BUCKETS - 1 - REL_EXACT) / (REL_BUCKETS - REL_EXACT))
    blk = _pick(seq, (256, 128))
    assert rel_sat < blk and rel_sat < page

    r = jnp.arange(blk, dtype=jnp.int32)
    rel_t = r[None, :] - r[:, None]
    causal = rel_t >= 0
    diag = jnp.where(causal[None], _rel_bias_values(rel_bias, rel_t), NEG)
    off1 = _rel_bias_values(rel_bias, rel_t + blk)
    far = jnp.broadcast_to(_rel_bias_values(rel_bias, jnp.full((1, 1), 2 * blk, jnp.int32)), off1.shape)
    zeros = jnp.zeros((1, blk, blk), F32)
    bias_p = jnp.stack([jnp.concatenate([diag, jnp.where(causal[None], 0.0, NEG)], axis=0),
                        jnp.concatenate([off1, zeros], axis=0),
                        jnp.concatenate([far, zeros], axis=0)])

    n_per_step = _pick(n_pages, (16, 8, 4, 2, 1))
    rows_d = 2 * A_HEADS * dec
    rep = lambda v: jnp.broadcast_to(v[:, None], (A_HEADS, 2) + v.shape[1:]).reshape((rows_d,) + v.shape[2:])
    far_s = jnp.broadcast_to(_rel_bias_values(rel_bias, jnp.full((1, 1), past, jnp.int32)),
                             (A_HEADS, dec, n_per_step * page))
    kpos_last = past - page + jnp.arange(page, dtype=jnp.int32)
    last = _rel_bias_values(rel_bias, pos_s[:, None] - kpos_last[None, :])
    bias_s = jnp.stack([rep(far_s), rep(jnp.concatenate([far_s[..., :(n_per_step - 1) * page], last], axis=-1))])
    rd = jnp.arange(dec, dtype=jnp.int32)
    rel_n = rd[:, None] - rd[None, :]
    bias_n = rep(jnp.where(rel_n[None] >= 0, _rel_bias_values(rel_bias, rel_n), NEG))
    mask_n = jnp.tile(jnp.where(rel_n >= 0, 0.0, NEG).astype(F32), (B_HEADS, 1))

    cache_kv2 = cache_diff_kv.reshape(depth * n_pool * page * 2 * A_HEADS, A_DV)
    cache_mla_t = jnp.swapaxes(cache_mla, 2, 3).reshape(depth * n_pool, mla_row, page)

    tm = _pick(seq, (512, 256, 128))
    tm_s = _pick(sb * dec, (512, 256, 128))
    tb = _pick(batch * seq, (512, 256, 128))
    tb_s = _pick(sb * dec, (512, 256, 128))

    hp = x_prompt.reshape(batch * seq, d)
    hs = x_sample.reshape(sb * dec, d)
    kv_p, mla_p, kv_s, mla_s = [], [], [], []
    for l in range(depth):
        lam_init = 0.8 - 0.6 * math.exp(-0.3 * l)
        w = w_in[l]
        o = 3 * aw + q_lora + kv_lora
        kr_w = w[:, o:]
        win = jnp.concatenate([w[:, :o], _pad_lanes(kr_w), _pad_lanes(_swap_halves(kr_w))], axis=-1).astype(BF16)
        uq = mla_w_uq[l]
        uq_r = uq[..., B_DN:]
        wuq = jnp.concatenate([uq[..., :B_DN].reshape(q_lora, -1), _pad_lanes(uq_r).reshape(q_lora, -1),
                               _pad_lanes(_swap_halves(uq_r)).reshape(q_lora, -1)], axis=-1).astype(BF16)
        wuk = jnp.transpose(mla_w_uk[l], (1, 2, 0)).astype(BF16)
        wuv = jnp.transpose(mla_w_uv[l], (1, 0, 2)).astype(BF16)
        wuvt = jnp.transpose(mla_w_uv[l], (1, 2, 0)).astype(BF16)
        qg = mla_q_norm_g[l].reshape(1, -1)
        kvg = mla_kv_norm_g[l].reshape(1, -1)
        subg = diff_subln_g[l].reshape(1, -1)
        subg_col = diff_subln_g[l].reshape(-1, 1)
        lam_p = diff_lambda[l]
        wout = w_out[l].astype(BF16)
        wq = peer_w_q[l].astype(BF16)
        sk = peer_sub_keys[l].astype(BF16)
        u_tab = peer_u[l].astype(BF16)
        v_tab = peer_v[l].astype(BF16)
        g1, b1 = ln1_g[l].reshape(1, -1), ln1_b[l].reshape(1, -1)
        g2, b2 = ln2_g[l].reshape(1, -1), ln2_b[l].reshape(1, -1)

        def post(h2d, mix, tb_, mix_transposed):
            h1, h1b, sa, sc, sbb, sr = _post_select(mix, h2d, wout, g1, b1, wq, sk, tb=tb_, alpha=alpha,
                                                    mix_transposed=mix_transposed)
            return _peer_dense(h1, h1b, sa, sc, sbb, sr, u_tab, v_tab, g2, b2, tb=tb_, rows_per_step=4,
                               alpha=alpha)

        kv_rows, mla_t, qd, kd, _, vdt, qm, km, ct = _project(
            hp, cos_p, sin_p, win, qg, wuq, kvg, wuk, groups=batch, tm=tm, q_lora=q_lora, kv_lora=kv_lora)
        mix_t = _prompt_attn(qd, qm, kd, vdt, km, ct, bias_p, lam_p, subg_col, wuvt, batch=batch, seq=seq,
                             blk=blk, lam_init=lam_init, kv_lora=kv_lora)
        kv_p.append(kv_rows.reshape(batch, seq, 2, A_HEADS, A_DV))
        mla_p.append(jnp.swapaxes(mla_t, 1, 2))
        hp = post(hp, mix_t, tb, True)

        kv_rows, mla_t, qd, kd, vd, _, qm, km, _ = _project(
            hs, cos_s, sin_s, win, qg, wuq, kvg, wuk, groups=1, tm=tm_s, q_lora=q_lora, kv_lora=kv_lora)
        r3 = lambda a: a.reshape(sb, dec, a.shape[-1])
        mix = _sample_attn(page_table, r3(qd), r3(qm), r3(kd), r3(vd), r3(km), cache_kv2, cache_mla_t,
                           bias_s, bias_n, mask_n, lam_p, subg, wuv, layer_off=l * n_pool, lam_init=lam_init,
                           kv_lora=kv_lora, n_per_step=n_per_step)
        kv_s.append(kv_rows.reshape(sb, dec, 2, A_HEADS, A_DV))
        mla_s.append(jnp.swapaxes(mla_t, 1, 2).reshape(sb, dec, mla_row))
        hs = post(hs, mix.reshape(sb * dec, -1), tb_s, False)

    return (hp.reshape(batch, seq, d), hs.reshape(sb, dec, d), jnp.stack(kv_p), jnp.stack(mla_p),
            jnp.stack(kv_s), jnp.stack(mla_s))
```

```python
import functools
import math

import numpy as np
import jax
import jax.numpy as jnp
from jax import lax
from jax.experimental import pallas as pl
from jax.experimental.pallas import tpu as pltpu

F32 = jnp.float32
BF16 = jnp.bfloat16

A_HEADS = 4
A_HALF = 64
A_DV = 128
A_WIDTH = A_HEADS * A_DV
B_HEADS = 4
B_DV = 128
B_DN = 128
B_DR = 64
ROPE_THETA = 10000.0
REL_BUCKETS = 32
REL_MAX_DIST = 128
REL_EXACT = REL_BUCKETS // 2
PEER_HEADS = 8
PEER_DK = 256
N_KEYS = 128
PEER_TOPK = 16
NEG = -1e30
LOG2E = math.log2(math.e)

SUBLANES = 8
LANES = 128
MLA_PAD = 384
VMEM_LIMIT = 48 * 1024 * 1024


def _cparams(sem):
    return pltpu.CompilerParams(dimension_semantics=sem, vmem_limit_bytes=VMEM_LIMIT)


def _full(shape):
    n = len(shape)
    return pl.BlockSpec(shape, lambda *_: (0,) * n)


def _t5_bucket(rel):
    n = jnp.maximum(rel, 0)
    large = REL_EXACT + (jnp.log(jnp.maximum(n, 1).astype(F32) / REL_EXACT)
                         / math.log(REL_MAX_DIST / REL_EXACT) * (REL_BUCKETS - REL_EXACT)).astype(jnp.int32)
    large = jnp.minimum(large, REL_BUCKETS - 1)
    return jnp.where(n < REL_EXACT, n, large)


def _nt(a, b):
    return lax.dot_general(a, b, (((1,), (1,)), ((), ())), preferred_element_type=F32)


def _tn(a, b):
    return lax.dot_general(a, b, (((0,), (0,)), ((), ())), preferred_element_type=F32)


def _dot(a, b):
    return jnp.dot(a, b, preferred_element_type=F32)


def _project_kernel(x_ref, win_ref, qg_ref, wuq_ref, kvg_ref, wuk_ref, cos_ref, sin_ref,
                    kv_out, mlat_out, qd_out, kd_out, vd_out, vdt_out, qm_out, km_out, ct_out,
                    *, a_scale, mla_scale, q_lora, kv_lora):
    tm = x_ref.shape[0]
    x = x_ref[...].astype(BF16)
    z = _dot(x, win_ref[...])
    aw = A_WIDTH
    for j in range(2 * A_HEADS):
        kv_out[pl.ds(j, tm, stride=2 * A_HEADS), :] = z[:, aw + j * A_DV:aw + (j + 1) * A_DV]
    qd_out[...] = (z[:, :aw] * a_scale).astype(BF16)
    kd_out[...] = z[:, aw:2 * aw].astype(BF16)
    v = z[:, 2 * aw:3 * aw]
    vd_out[...] = v.astype(BF16)
    vdt_out[...] = v.T.astype(BF16)
    o = 3 * aw
    cq = z[:, o:o + q_lora]
    o += q_lora
    ckv = z[:, o:o + kv_lora]
    o += kv_lora
    kr = z[:, o:o + LANES]
    krs = z[:, o + LANES:o + 2 * LANES]
    cos = cos_ref[...]
    sin = sin_ref[...]

    cqn = cq * lax.rsqrt(jnp.mean(cq * cq, axis=-1, keepdims=True) + 1e-6) * qg_ref[...]
    q = _dot(cqn.astype(BF16), wuq_ref[...])
    ckvn = ckv * lax.rsqrt(jnp.mean(ckv * ckv, axis=-1, keepdims=True) + 1e-6) * kvg_ref[...]
    krr = kr * cos + krs * sin

    km_out[:, :kv_lora] = ckvn.astype(BF16)
    km_out[:, kv_lora:] = krr.astype(BF16)
    mla_t = jnp.concatenate([ckvn, krr], axis=1).T
    mlat_out[0] = mla_t[:kv_lora + B_DR]
    ct_out[...] = mla_t[:kv_lora].astype(BF16)

    nope_w = B_HEADS * B_DN
    rope_w = B_HEADS * LANES
    for h in range(B_HEADS):
        ql = _dot(q[:, h * B_DN:(h + 1) * B_DN].astype(BF16), wuk_ref[h])
        qr = (q[:, nope_w + h * LANES:nope_w + (h + 1) * LANES] * cos
              + q[:, nope_w + rope_w + h * LANES:nope_w + rope_w + (h + 1) * LANES] * sin)
        base = h * MLA_PAD
        qm_out[:, base:base + kv_lora] = (ql * mla_scale).astype(BF16)
        qm_out[:, base + kv_lora:base + MLA_PAD] = (qr * mla_scale).astype(BF16)


def _project(x2d, cos, sin, win, qg, wuq, kvg, wuk, *, groups, tm, q_lora, kv_lora):
    t, d = x2d.shape
    aw = A_WIDTH
    mla_row = kv_lora + B_DR
    tiles_per_group = t // groups // tm
    kern = functools.partial(_project_kernel, a_scale=A_HALF ** -0.5 * LOG2E,
                             mla_scale=(B_DN + B_DR) ** -0.5 * LOG2E, q_lora=q_lora, kv_lora=kv_lora)
    row = lambda w: pl.BlockSpec((tm, w), lambda i: (i, 0))
    col = lambda w: pl.BlockSpec((w, tm), lambda i: (0, i))
    return pl.pallas_call(
        kern,
        grid=(t // tm,),
        in_specs=[row(d), _full(win.shape), _full(qg.shape), _full(wuq.shape), _full(kvg.shape),
                  _full(wuk.shape), row(LANES), row(LANES)],
        out_specs=[pl.BlockSpec((tm * 2 * A_HEADS, A_DV), lambda i: (i, 0)),
                   pl.BlockSpec((1, mla_row, tm), lambda i: (i // tiles_per_group, 0, i % tiles_per_group)),
                   row(aw), row(aw), row(aw), col(aw), row(B_HEADS * MLA_PAD), row(MLA_PAD), col(kv_lora)],
        out_shape=[jax.ShapeDtypeStruct((t * 2 * A_HEADS, A_DV), F32),
                   jax.ShapeDtypeStruct((groups, mla_row, t // groups), F32),
                   jax.ShapeDtypeStruct((t, aw), BF16), jax.ShapeDtypeStruct((t, aw), BF16),
                   jax.ShapeDtypeStruct((t, aw), BF16), jax.ShapeDtypeStruct((aw, t), BF16),
                   jax.ShapeDtypeStruct((t, B_HEADS * MLA_PAD), BF16),
                   jax.ShapeDtypeStruct((t, MLA_PAD), BF16), jax.ShapeDtypeStruct((kv_lora, t), BF16)],
        compiler_params=_cparams(("parallel",)),
        name="project",
    )(x2d, win, qg, wuq, kvg, wuk, cos, sin)


def _lambda_value(lam_ref, lam_init):
    lp = lam_ref[...]
    l01 = jnp.sum(lp[0:1] * lp[1:2], axis=-1, keepdims=True)
    l23 = jnp.sum(lp[2:3] * lp[3:4], axis=-1, keepdims=True)
    return jnp.exp(l01) - jnp.exp(l23) + lam_init


def _update_t(s_t, pv, m_ref, l_ref, acc_ref):
    m_prev = m_ref[...]
    m_new = jnp.maximum(m_prev, jnp.max(s_t, axis=0, keepdims=True))
    alpha = jnp.exp2(m_prev - m_new)
    e = jnp.exp2(s_t - m_new)
    l_ref[...] = alpha * l_ref[...] + jnp.sum(e, axis=0, keepdims=True)
    acc_ref[...] = alpha * acc_ref[...] + pv(e.astype(BF16))
    m_ref[...] = m_new


def _prompt_attn_kernel(qi_ref, kj_ref, qd_ref, qm_ref, kd_ref, vdt_ref, km_ref, ct_ref, bias_ref, lam_ref,
                        subg_ref, wuvt_ref, o_ref, m_d, l_d, acc_d, m_m, l_m, acc_m, *, lam_init, ratio):
    p = pl.program_id(1)
    qi = qi_ref[p]
    kj = kj_ref[p]

    @pl.when(kj == 0)
    def _():
        m_d[...] = jnp.full(m_d.shape, -jnp.inf, F32)
        l_d[...] = jnp.zeros(l_d.shape, F32)
        acc_d[...] = jnp.zeros(acc_d.shape, F32)
        m_m[...] = jnp.full(m_m.shape, -jnp.inf, F32)
        l_m[...] = jnp.zeros(l_m.shape, F32)
        acc_m[...] = jnp.zeros(acc_m.shape, F32)

    bq = qd_ref.shape[0]
    lane = lax.broadcasted_iota(jnp.int32, (bq, A_DV), 1)
    parts, biases = [], []
    for h in range(A_HEADS):
        hs = slice(h * A_DV, (h + 1) * A_DV)
        qh = qd_ref[:, hs]
        zero = jnp.zeros_like(qh)
        q2 = jnp.concatenate([jnp.where(lane < A_HALF, qh, zero), jnp.where(lane >= A_HALF, qh, zero)], axis=0)
        parts.append(_nt(kd_ref[:, hs], q2))
        biases += [bias_ref[0, h]] * 2
    s_t = jnp.concatenate(parts, axis=1) + jnp.concatenate(biases, axis=1)

    def pv_d(e):
        return jnp.concatenate([_dot(vdt_ref[h * A_DV:(h + 1) * A_DV, :], e[:, 2 * h * bq:2 * (h + 1) * bq])
                                for h in range(A_HEADS)], axis=1)

    _update_t(s_t, pv_d, m_d, l_d, acc_d)

    q_rows = jnp.concatenate([qm_ref[:, h * MLA_PAD:(h + 1) * MLA_PAD] for h in range(B_HEADS)], axis=0)
    s_t = _nt(km_ref[...], q_rows) + jnp.concatenate([bias_ref[0, A_HEADS]] * B_HEADS, axis=1)
    _update_t(s_t, lambda e: _dot(ct_ref[...], e), m_m, l_m, acc_m)

    @pl.when(kj == ratio * qi + (ratio - 1))
    def _():
        lam = _lambda_value(lam_ref, lam_init)
        o_d = acc_d[...] / l_d[...]
        for h in range(A_HEADS):
            o_a = o_d[:, 2 * h * bq:(2 * h + 1) * bq]
            o_b = o_d[:, (2 * h + 1) * bq:(2 * h + 2) * bq]
            w = o_a - lam * o_b
            w = w * lax.rsqrt(jnp.mean(w * w, axis=0, keepdims=True) + 1e-5) * subg_ref[...] * (1.0 - lam_init)
            o_ref[h * A_DV:(h + 1) * A_DV, :] = w.astype(o_ref.dtype)
        o_m = (acc_m[...] / l_m[...]).astype(BF16)
        for h in range(B_HEADS):
            o = _dot(wuvt_ref[h], o_m[:, h * bq:(h + 1) * bq])
            o_ref[A_WIDTH + h * B_DV:A_WIDTH + (h + 1) * B_DV, :] = o.astype(o_ref.dtype)


def _prompt_attn(qd, qm, kd, vdt, km, ct, bias_t, lam_p, subg_col, wuvt, *, batch, seq, bq, bk, lam_init, kv_lora):
    nq, nk, ratio = seq // bq, seq // bk, bq // bk
    far = bias_t.shape[0] - 1
    pairs = [(i, j) for i in range(nq) for j in range(ratio * i + ratio)]
    qi_arr = jnp.asarray(np.array([p[0] for p in pairs], np.int32))
    kj_arr = jnp.asarray(np.array([p[1] for p in pairs], np.int32))
    mix_w = A_WIDTH + B_HEADS * B_DV
    qmap = lambda b, p, qi, kj: (b * nq + qi[p], 0)
    kmap = lambda b, p, qi, kj: (b * nk + kj[p], 0)
    kmap_t = lambda b, p, qi, kj: (0, b * nk + kj[p])
    bmap = lambda b, p, qi, kj: (jnp.minimum(ratio * qi[p] + (ratio - 1) - kj[p], far), 0, 0, 0)
    cst = lambda shape: pl.BlockSpec(shape, lambda b, p, qi, kj: (0,) * len(shape))
    grid_spec = pltpu.PrefetchScalarGridSpec(
        num_scalar_prefetch=2,
        grid=(batch, len(pairs)),
        in_specs=[pl.BlockSpec((bq, A_WIDTH), qmap), pl.BlockSpec((bq, B_HEADS * MLA_PAD), qmap),
                  pl.BlockSpec((bk, A_WIDTH), kmap), pl.BlockSpec((A_WIDTH, bk), kmap_t),
                  pl.BlockSpec((bk, MLA_PAD), kmap), pl.BlockSpec((kv_lora, bk), kmap_t),
                  pl.BlockSpec((1, A_HEADS + 1, bk, bq), bmap),
                  cst(lam_p.shape), cst(subg_col.shape), cst(wuvt.shape)],
        out_specs=pl.BlockSpec((mix_w, bq), lambda b, p, qi, kj: (0, b * nq + qi[p])),
        scratch_shapes=[pltpu.VMEM((1, 2 * A_HEADS * bq), F32), pltpu.VMEM((1, 2 * A_HEADS * bq), F32),
                        pltpu.VMEM((A_DV, 2 * A_HEADS * bq), F32),
                        pltpu.VMEM((1, B_HEADS * bq), F32), pltpu.VMEM((1, B_HEADS * bq), F32),
                        pltpu.VMEM((kv_lora, B_HEADS * bq), F32)],
    )
    return pl.pallas_call(
        functools.partial(_prompt_attn_kernel, lam_init=lam_init, ratio=ratio),
        grid_spec=grid_spec,
        out_shape=jax.ShapeDtypeStruct((mix_w, batch * seq), BF16),
        compiler_params=_cparams(("parallel", "arbitrary")),
        name="prompt_attn",
    )(qi_arr, kj_arr, qd, qm, kd, vdt, km, ct, bias_t, lam_p, subg_col, wuvt)


def _update(s, pv, m_ref, l_ref, acc_ref):
    m_prev = m_ref[...]
    m_new = jnp.maximum(m_prev, jnp.max(s, axis=-1, keepdims=True))
    alpha = jnp.exp2(m_prev - m_new)
    e = jnp.exp2(s - m_new)
    l_ref[...] = alpha * l_ref[...] + jnp.sum(e, axis=-1, keepdims=True)
    acc_ref[...] = alpha * acc_ref[...] + pv(e.astype(BF16))
    m_ref[...] = m_new


def _sample_attn_kernel(pt_ref, qd_ref, qm_ref, kdn_ref, vdn_ref, kmn_ref, bias_ref, biasn_ref, maskn_ref,
                        lam_ref, subg_ref, wuv_ref, *rest, lam_init, kv_lora, n_per_step):
    kv_refs = rest[:n_per_step]
    mla_refs = rest[n_per_step:2 * n_per_step]
    o_ref, qbd, qmm, m_d, l_d, acc_d, m_m, l_m, acc_m = rest[2 * n_per_step:]
    p = pl.program_id(1)
    dec = qd_ref.shape[1]
    n_maps = 2 * A_HEADS
    mla_row = kv_lora + B_DR
    page = mla_refs[0].shape[2]

    @pl.when(p == 0)
    def _():
        q8 = qd_ref[0].astype(F32)
        q_rep = jnp.concatenate([q8] * n_maps, axis=0)
        row = lax.broadcasted_iota(jnp.int32, q_rep.shape, 0)
        lane = lax.broadcasted_iota(jnp.int32, q_rep.shape, 1)
        qbd[...] = jnp.where(lane // A_HALF == row // dec, q_rep, 0.0).astype(BF16)
        qm8 = qm_ref[0].astype(F32)
        qmm[...] = jnp.concatenate([qm8[:, h * MLA_PAD:h * MLA_PAD + mla_row] for h in range(B_HEADS)],
                                   axis=0).astype(BF16)
        m_d[...] = jnp.full(m_d.shape, -jnp.inf, F32)
        l_d[...] = jnp.zeros(l_d.shape, F32)
        acc_d[...] = jnp.zeros(acc_d.shape, F32)
        m_m[...] = jnp.full(m_m.shape, -jnp.inf, F32)
        l_m[...] = jnp.zeros(l_m.shape, F32)
        acc_m[...] = jnp.zeros(acc_m.shape, F32)

    q_d = qbd[...]
    q_m = qmm[...]
    heads = lambda ref, off: jnp.concatenate(
        [ref[pl.ds(off + h, page, stride=2 * A_HEADS), :].astype(BF16) for h in range(A_HEADS)], axis=1)
    k_all = jnp.concatenate([heads(r, 0) for r in kv_refs], axis=0)
    v_all = jnp.concatenate([heads(r, A_HEADS) for r in kv_refs], axis=0)
    r_all = jnp.concatenate([r[0].astype(BF16) for r in mla_refs], axis=1)

    _update(_nt(q_d, k_all) + bias_ref[0], lambda e: _dot(e, v_all), m_d, l_d, acc_d)
    _update(_dot(q_m, r_all), lambda e: _nt(e, r_all[:kv_lora]), m_m, l_m, acc_m)

    @pl.when(p == pl.num_programs(1) - 1)
    def _():
        vdn = vdn_ref[0]
        _update(_nt(q_d, kdn_ref[0]) + biasn_ref[...], lambda e: _dot(e, vdn), m_d, l_d, acc_d)
        kmn = kmn_ref[0]
        _update(_nt(q_m, kmn[:, :mla_row]) + maskn_ref[...], lambda e: _dot(e, kmn[:, :kv_lora]), m_m, l_m, acc_m)

        lam = _lambda_value(lam_ref, lam_init)
        o_all = acc_d[...] / l_d[...]
        for h in range(A_HEADS):
            ra = (2 * h) * dec
            rb = (2 * h + 1) * dec
            o_a = o_all[ra:ra + dec, h * A_DV:(h + 1) * A_DV]
            o_b = o_all[rb:rb + dec, h * A_DV:(h + 1) * A_DV]
            w = o_a - lam * o_b
            w = w * lax.rsqrt(jnp.mean(w * w, axis=-1, keepdims=True) + 1e-5) * subg_ref[...] * (1.0 - lam_init)
            o_ref[0, :, h * A_DV:(h + 1) * A_DV] = w.astype(o_ref.dtype)
        ol_all = acc_m[...] / l_m[...]
        for h in range(B_HEADS):
            ol = ol_all[h * dec:(h + 1) * dec]
            o = _dot(ol.astype(BF16), wuv_ref[h])
            o_ref[0, :, A_WIDTH + h * B_DV:A_WIDTH + (h + 1) * B_DV] = o.astype(o_ref.dtype)


def _sample_attn(page_table, qd, qm, kdn, vdn, kmn, cache_kv, cache_mla_t, bias, biasn, maskn, lam_p, subg, wuv,
                 *, layer_off, lam_init, kv_lora, n_per_step):
    sb, dec, _ = qd.shape
    n_pages = page_table.shape[1]
    mla_row, page = cache_mla_t.shape[1], cache_mla_t.shape[2]
    mix_w = A_WIDTH + B_HEADS * B_DV
    steps = n_pages // n_per_step
    rows_d = 2 * A_HEADS * dec
    rows_m = B_HEADS * dec
    pt_flat = page_table.reshape(-1)

    seq3 = lambda w: pl.BlockSpec((1, dec, w), lambda s, p, pt: (s, 0, 0))
    cst = lambda shape: pl.BlockSpec(shape, lambda s, p, pt: (0,) * len(shape))

    def page_id(j):
        return lambda s, p, pt: pt[s * n_pages + p * n_per_step + j] + layer_off

    in_specs = [seq3(A_WIDTH), seq3(B_HEADS * MLA_PAD), seq3(A_WIDTH), seq3(A_WIDTH), seq3(MLA_PAD),
                pl.BlockSpec((1, rows_d, n_per_step * page),
                             lambda s, p, pt: (jnp.where(p == steps - 1, 1, 0), 0, 0)),
                cst(biasn.shape), cst(maskn.shape), cst(lam_p.shape), cst(subg.shape), cst(wuv.shape)]
    for j in range(n_per_step):
        in_specs.append(pl.BlockSpec((page * 2 * A_HEADS, A_DV), lambda s, p, pt, f=page_id(j): (f(s, p, pt), 0)))
    for j in range(n_per_step):
        in_specs.append(pl.BlockSpec((1, mla_row, page), lambda s, p, pt, f=page_id(j): (f(s, p, pt), 0, 0)))
    kern = functools.partial(_sample_attn_kernel, lam_init=lam_init, kv_lora=kv_lora, n_per_step=n_per_step)
    grid_spec = pltpu.PrefetchScalarGridSpec(
        num_scalar_prefetch=1,
        grid=(sb, steps),
        in_specs=in_specs,
        out_specs=pl.BlockSpec((1, dec, mix_w), lambda s, p, pt: (s, 0, 0)),
        scratch_shapes=[pltpu.VMEM((rows_d, A_WIDTH), BF16), pltpu.VMEM((rows_m, mla_row), BF16),
                        pltpu.VMEM((rows_d, 1), F32), pltpu.VMEM((rows_d, 1), F32),
                        pltpu.VMEM((rows_d, A_WIDTH), F32),
                        pltpu.VMEM((rows_m, 1), F32), pltpu.VMEM((rows_m, 1), F32),
                        pltpu.VMEM((rows_m, kv_lora), F32)],
    )
    return pl.pallas_call(
        kern,
        grid_spec=grid_spec,
        out_shape=jax.ShapeDtypeStruct((sb, dec, mix_w), BF16),
        compiler_params=_cparams(("parallel", "arbitrary")),
        name="sample_attn",
    )(pt_flat, qd, qm, kdn, vdn, kmn, bias, biasn, maskn, lam_p, subg, wuv,
      *([cache_kv] * n_per_step), *([cache_mla_t] * n_per_step))


def _layer_norm(x, g, b):
    xc = x - jnp.mean(x, axis=-1, keepdims=True)
    var = jnp.mean(xc * xc, axis=-1, keepdims=True)
    return xc * lax.rsqrt(var + 1e-5) * g + b


def _extract_topk(s, k_top, want_rank):
    n, t = s.shape
    iota = lax.broadcasted_iota(jnp.int32, (n, t), 0).astype(F32)
    iota_k = lax.broadcasted_iota(jnp.int32, (k_top, t), 0)
    rank = jnp.full((n, t), float(n), F32) if want_rank else None
    vals = jnp.zeros((k_top, t), F32)
    picks = []
    for k in range(k_top):
        m = jnp.max(s, axis=0, keepdims=True)
        idx = jnp.min(jnp.where(s == m, iota, float(n)), axis=0, keepdims=True)
        sel = iota == idx
        if want_rank:
            rank = jnp.where(sel, float(k), rank)
        s = jnp.where(sel, -jnp.inf, s)
        vals = jnp.where(iota_k == k, m, vals)
        picks.append(idx)
    return rank, vals, picks, s


def _post_select_kernel(mix_ref, x_ref, wout_ref, g_ref, b_ref, wq_ref, sk_ref,
                        h_out, hb_out, a_out, cnt_out, b_out, r2_out, *, alpha, mix_transposed):
    a = _tn(mix_ref[...], wout_ref[...]) if mix_transposed else _dot(mix_ref[...], wout_ref[...])
    h = _layer_norm(alpha * x_ref[...] + a, g_ref[...], b_ref[...])
    h_out[...] = h
    hb = h.astype(BF16)
    hb_out[...] = hb
    q = _dot(hb, wq_ref[...])
    half = PEER_DK // 2
    kk = PEER_TOPK
    sub = kk // 2
    iota_n = lax.broadcasted_iota(jnp.int32, (N_KEYS, q.shape[0]), 0).astype(F32)
    for hd in range(PEER_HEADS):
        q1 = q[:, hd * PEER_DK:hd * PEER_DK + half].astype(BF16)
        q2 = q[:, hd * PEER_DK + half:(hd + 1) * PEER_DK].astype(BF16)
        s1 = _nt(sk_ref[0], q1)
        s2 = _nt(sk_ref[1], q2)
        _, v1, picks1, s1_left = _extract_topk(s1, kk, False)
        r2, v2, _, _ = _extract_topk(s2, kk, True)
        cand = jnp.concatenate([v1[0:1] + v2] + [v1[k:k + 1] + v2[:sub] for k in range(1, kk)], axis=0)
        _, _, _, cand_left = _extract_topk(cand, kk, False)
        picked = cand_left == -jnp.inf
        e = jnp.where(picked, jnp.exp(cand - (v1[0:1] + v2[0:1])), 0.0)
        z = jnp.sum(e, axis=0, keepdims=True)
        pf = picked.astype(F32)
        cnt = jnp.zeros_like(s1)
        for k in range(kk):
            rows = pf[0:kk] if k == 0 else pf[kk + (k - 1) * sub:kk + k * sub]
            c_k = jnp.sum(rows, axis=0, keepdims=True)
            cnt = jnp.where(iota_n == picks1[k], c_k, cnt)
        a_out[hd] = jnp.where(s1_left == -jnp.inf, jnp.exp(s1 - v1[0:1]), 0.0) / z
        cnt_out[hd] = cnt
        b_out[hd] = jnp.where(r2 < float(kk), jnp.exp(s2 - v2[0:1]), 0.0).astype(b_out.dtype)
        r2_out[hd] = r2.astype(r2_out.dtype)


def _post_select(mix, x2d, wout, g, b, wq, sk, *, tb, alpha, mix_transposed):
    t, d = x2d.shape
    row = lambda w: pl.BlockSpec((tb, w), lambda i: (i, 0))
    mix_spec = pl.BlockSpec((mix.shape[0], tb), lambda i: (0, i)) if mix_transposed else row(mix.shape[1])
    sel = pl.BlockSpec((PEER_HEADS, N_KEYS, tb), lambda i: (0, 0, i))
    sel_shape = jax.ShapeDtypeStruct((PEER_HEADS, N_KEYS, t), F32)
    sel_half = jax.ShapeDtypeStruct((PEER_HEADS, N_KEYS, t), BF16)
    return pl.pallas_call(
        functools.partial(_post_select_kernel, alpha=alpha, mix_transposed=mix_transposed),
        grid=(t // tb,),
        in_specs=[mix_spec, row(d), _full(wout.shape), _full(g.shape), _full(b.shape),
                  _full(wq.shape), _full(sk.shape)],
        out_specs=[row(d), row(d), sel, sel, sel, sel],
        out_shape=[jax.ShapeDtypeStruct((t, d), F32), jax.ShapeDtypeStruct((t, d), BF16),
                   sel_shape, sel_shape, sel_half, sel_half],
        compiler_params=_cparams(("parallel",)),
        name="post_select",
    )(mix, x2d, wout, g, b, wq, sk)


def _peer_dense_kernel(h_ref, hb_ref, a_ref, cnt_ref, b_ref, r2_ref, u_ref, v_ref, g_ref, bb_ref,
                       o_ref, acc, s_scr, w_scr, *, alpha, rows_per_step):
    e = pl.program_id(1)

    @pl.when(e == 0)
    def _():
        acc[...] = jnp.zeros(acc.shape, F32)

    s_scr[...] = _nt(u_ref[...], hb_ref[...])

    tb = hb_ref.shape[0]
    pack = 2 * SUBLANES
    spread = lambda row: jnp.concatenate([jnp.broadcast_to(row, (pack, tb)).astype(BF16)] * (N_KEYS // pack), axis=0)

    def row_body(r, carry):
        i = e * rows_per_step + r
        rows = pl.ds(pl.multiple_of(r * N_KEYS, N_KEYS), N_KEYS)
        s = s_scr[rows, :]
        act = (0.5 * s * (1.0 + lax.erf(s * (2.0 ** -0.5)))).astype(BF16)
        zero = jnp.zeros((N_KEYS, tb), BF16)
        gate = zero
        for hd in range(PEER_HEADS):
            a_b = spread(a_ref[hd, pl.ds(i, 1), :])
            c_b = spread(cnt_ref[hd, pl.ds(i, 1), :])
            gate = gate + jnp.where(r2_ref[hd] < c_b, b_ref[hd], zero) * a_b
        w_scr[rows, :] = gate * act
        return carry

    lax.fori_loop(0, rows_per_step, row_body, 0)
    acc[...] += _tn(w_scr[...], v_ref[...])

    @pl.when(e == pl.num_programs(1) - 1)
    def _():
        o_ref[...] = _layer_norm(alpha * h_ref[...] + acc[...], g_ref[...], bb_ref[...])


def _peer_dense(h, hb, sel_a, sel_cnt, sel_b, sel_r2, u_tab, v_tab, g, b, *, tb, rows_per_step, alpha):
    t, d = h.shape
    n_exp = u_tab.shape[0]
    eb = rows_per_step * N_KEYS
    row = lambda w: pl.BlockSpec((tb, w), lambda i, e: (i, 0))
    sel = pl.BlockSpec((PEER_HEADS, N_KEYS, tb), lambda i, e: (0, 0, i))
    tab = pl.BlockSpec((eb, d), lambda i, e: (e, 0))
    cst = lambda shape: pl.BlockSpec(shape, lambda i, e: (0,) * len(shape))
    return pl.pallas_call(
        functools.partial(_peer_dense_kernel, alpha=alpha, rows_per_step=rows_per_step),
        grid=(t // tb, n_exp // eb),
        in_specs=[row(d), row(d), sel, sel, sel, sel, tab, tab, cst(g.shape), cst(b.shape)],
        out_specs=row(d),
        out_shape=jax.ShapeDtypeStruct((t, d), F32),
        scratch_shapes=[pltpu.VMEM((tb, d), F32), pltpu.VMEM((eb, tb), F32), pltpu.VMEM((eb, tb), BF16)],
        compiler_params=_cparams(("parallel", "arbitrary")),
        name="peer_dense",
    )(h, hb, sel_a, sel_cnt, sel_b, sel_r2, u_tab, v_tab, g, b)


def _rope_tables(pos):
    inv = 1.0 / (ROPE_THETA ** (jnp.arange(0, B_DR, 2, dtype=F32) / B_DR))
    ang = pos.astype(F32)[:, None] * inv[None, :]
    cos, sin = jnp.cos(ang), jnp.sin(ang)
    pad = jnp.zeros((pos.shape[0], LANES - B_DR), F32)
    return (jnp.concatenate([cos, cos, pad], axis=-1), jnp.concatenate([-sin, sin, pad], axis=-1))


def _swap_halves(w):
    half = w.shape[-1] // 2
    return jnp.concatenate([w[..., half:], w[..., :half]], axis=-1)


def _pad_lanes(w):
    return jnp.pad(w, [(0, 0)] * (w.ndim - 1) + [(0, LANES - w.shape[-1])])


def _pick(n, prefs):
    for c in prefs:
        if n % c == 0:
            return c
    return n


def _rel_bias_values(rel_bias, rel):
    onehot = _t5_bucket(rel)[..., None, None] == jnp.arange(REL_BUCKETS)[:, None]
    vals = jnp.sum(jnp.where(onehot, rel_bias.astype(F32), 0.0), axis=-2)
    return jnp.moveaxis(vals, -1, 0) * LOG2E


def kernel(x_prompt, x_sample, cache_diff_kv, cache_mla, page_table, w_in, diff_lambda, diff_subln_g, rel_bias,
           mla_q_norm_g, mla_w_uq, mla_kv_norm_g, mla_w_uk, mla_w_uv, w_out, ln1_g, ln1_b, peer_w_q,
           peer_sub_keys, peer_u, peer_v, ln2_g, ln2_b):
    batch, seq, d = x_prompt.shape
    sb, dec, _ = x_sample.shape
    depth = w_in.shape[0]
    n_pool, page = cache_mla.shape[1], cache_mla.shape[2]
    n_pages = page_table.shape[1]
    past = n_pages * page
    q_lora = mla_q_norm_g.shape[1]
    kv_lora = mla_kv_norm_g.shape[1]
    mla_row = kv_lora + B_DR
    alpha = (2 * depth) ** 0.25
    aw = A_WIDTH

    pos_p = jnp.arange(seq, dtype=jnp.int32)
    pos_s = past + jnp.arange(dec, dtype=jnp.int32)
    cos_p, sin_p = _rope_tables(jnp.tile(pos_p, batch))
    cos_s, sin_s = _rope_tables(jnp.tile(pos_s, sb))

    rel_sat = REL_EXACT * (REL_MAX_DIST / REL_EXACT) ** ((REL_BUCKETS - 1 - REL_EXACT) / (REL_BUCKETS - REL_EXACT))
    bq = _pick(seq, (512, 256, 128))
    bk = _pick(seq, (256, 128))
    ratio = bq // bk
    assert rel_sat < page

    rel_t = jnp.arange(bq, dtype=jnp.int32)[None, :] - jnp.arange(bk, dtype=jnp.int32)[:, None]
    tiles = []
    t = 0
    while True:
        off = (t - (ratio - 1)) * bk
        rel = rel_t + off
        ok = rel >= 0
        tiles.append(jnp.concatenate([jnp.where(ok[None], _rel_bias_values(rel_bias, rel), NEG),
                                      jnp.where(ok, 0.0, NEG)[None]], axis=0))
        if off - (bk - 1) >= rel_sat:
            break
        t += 1
    bias_p = jnp.stack(tiles)

    n_per_step = _pick(n_pages, (16, 8, 4, 2, 1))
    rows_d = 2 * A_HEADS * dec
    rep = lambda v: jnp.broadcast_to(v[:, None], (A_HEADS, 2) + v.shape[1:]).reshape((rows_d,) + v.shape[2:])
    far_s = jnp.broadcast_to(_rel_bias_values(rel_bias, jnp.full((1, 1), past, jnp.int32)),
                             (A_HEADS, dec, n_per_step * page))
    kpos_last = past - page + jnp.arange(page, dtype=jnp.int32)
    last = _rel_bias_values(rel_bias, pos_s[:, None] - kpos_last[None, :])
    bias_s = jnp.stack([rep(far_s), rep(jnp.concatenate([far_s[..., :(n_per_step - 1) * page], last], axis=-1))])
    rd = jnp.arange(dec, dtype=jnp.int32)
    rel_n = rd[:, None] - rd[None, :]
    bias_n = rep(jnp.where(rel_n[None] >= 0, _rel_bias_values(rel_bias, rel_n), NEG))
    mask_n = jnp.tile(jnp.where(rel_n >= 0, 0.0, NEG).astype(F32), (B_HEADS, 1))

    cache_kv2 = cache_diff_kv.reshape(depth * n_pool * page * 2 * A_HEADS, A_DV)
    cache_mla_t = jnp.swapaxes(cache_mla, 2, 3).reshape(depth * n_pool, mla_row, page)

    tm = _pick(seq, (512, 256, 128))
    tm_s = _pick(sb * dec, (512, 256, 128))
    tb = _pick(batch * seq, (512, 256, 128))
    tb_s = _pick(sb * dec, (512, 256, 128))

    hp = x_prompt.reshape(batch * seq, d)
    hs = x_sample.reshape(sb * dec, d)
    kv_p, mla_p, kv_s, mla_s = [], [], [], []
    for l in range(depth):
        lam_init = 0.8 - 0.6 * math.exp(-0.3 * l)
        w = w_in[l]
        o = 3 * aw + q_lora + kv_lora
        kr_w = w[:, o:]
        win = jnp.concatenate([w[:, :o], _pad_lanes(kr_w), _pad_lanes(_swap_halves(kr_w))], axis=-1).astype(BF16)
        uq = mla_w_uq[l]
        uq_r = uq[..., B_DN:]
        wuq = jnp.concatenate([uq[..., :B_DN].reshape(q_lora, -1), _pad_lanes(uq_r).reshape(q_lora, -1),
                               _pad_lanes(_swap_halves(uq_r)).reshape(q_lora, -1)], axis=-1).astype(BF16)
        wuk = jnp.transpose(mla_w_uk[l], (1, 2, 0)).astype(BF16)
        wuv = jnp.transpose(mla_w_uv[l], (1, 0, 2)).astype(BF16)
        wuvt = jnp.transpose(mla_w_uv[l], (1, 2, 0)).astype(BF16)
        qg = mla_q_norm_g[l].reshape(1, -1)
        kvg = mla_kv_norm_g[l].reshape(1, -1)
        subg = diff_subln_g[l].reshape(1, -1)
        subg_col = diff_subln_g[l].reshape(-1, 1)
        lam_p = diff_lambda[l]
        wout = w_out[l].astype(BF16)
        wq = peer_w_q[l].astype(BF16)
        sk = peer_sub_keys[l].astype(BF16)
        u_tab = peer_u[l].astype(BF16)
        v_tab = peer_v[l].astype(BF16)
        g1, b1 = ln1_g[l].reshape(1, -1), ln1_b[l].reshape(1, -1)
        g2, b2 = ln2_g[l].reshape(1, -1), ln2_b[l].reshape(1, -1)

        def post(h2d, mix, tb_, mix_transposed):
            h1, h1b, sa, sc, sbb, sr = _post_select(mix, h2d, wout, g1, b1, wq, sk, tb=tb_, alpha=alpha,
                                                    mix_transposed=mix_transposed)
            return _peer_dense(h1, h1b, sa, sc, sbb, sr, u_tab, v_tab, g2, b2, tb=tb_, rows_per_step=8,
                               alpha=alpha)

        kv_rows, mla_t, qd, kd, _, vdt, qm, km, ct = _project(
            hp, cos_p, sin_p, win, qg, wuq, kvg, wuk, groups=batch, tm=tm, q_lora=q_lora, kv_lora=kv_lora)
        mix_t = _prompt_attn(qd, qm, kd, vdt, km, ct, bias_p, lam_p, subg_col, wuvt, batch=batch, seq=seq,
                             bq=bq, bk=bk, lam_init=lam_init, kv_lora=kv_lora)
        kv_p.append(kv_rows.reshape(batch, seq, 2, A_HEADS, A_DV))
        mla_p.append(jnp.swapaxes(mla_t, 1, 2))
        hp = post(hp, mix_t, tb, True)

        kv_rows, mla_t, qd, kd, vd, _, qm, km, _ = _project(
            hs, cos_s, sin_s, win, qg, wuq, kvg, wuk, groups=1, tm=tm_s, q_lora=q_lora, kv_lora=kv_lora)
        r3 = lambda a: a.reshape(sb, dec, a.shape[-1])
        mix = _sample_attn(page_table, r3(qd), r3(qm), r3(kd), r3(vd), r3(km), cache_kv2, cache_mla_t,
                           bias_s, bias_n, mask_n, lam_p, subg, wuv, layer_off=l * n_pool, lam_init=lam_init,
                           kv_lora=kv_lora, n_per_step=n_per_step)
        kv_s.append(kv_rows.reshape(sb, dec, 2, A_HEADS, A_DV))
        mla_s.append(jnp.swapaxes(mla_t, 1, 2).reshape(sb, dec, mla_row))
        hs = post(hs, mix.reshape(sb * dec, -1), tb_s, False)

    return (hp.reshape(batch, seq, d), hs.reshape(sb, dec, d), jnp.stack(kv_p), jnp.stack(mla_p),
            jnp.stack(kv_s), jnp.stack(mla_s))
```

```python
import functools
import math

import numpy as np
import jax
import jax.numpy as jnp
from jax import lax
from jax.experimental import pallas as pl
from jax.experimental.pallas import tpu as pltpu

F32 = jnp.float32
BF16 = jnp.bfloat16

A_HEADS = 4
A_HALF = 64
A_DV = 128
A_WIDTH = A_HEADS * A_DV
B_HEADS = 4
B_DV = 128
B_DN = 128
B_DR = 64
ROPE_THETA = 10000.0
REL_BUCKETS = 32
REL_MAX_DIST = 128
REL_EXACT = REL_BUCKETS // 2
PEER_HEADS = 8
PEER_DK = 256
N_KEYS = 128
PEER_TOPK = 16
NEG = -1e30
LOG2E = math.log2(math.e)

SUBLANES = 8
LANES = 128
MLA_PAD = 384
VMEM_LIMIT = 48 * 1024 * 1024


def _cparams(sem):
    return pltpu.CompilerParams(dimension_semantics=sem, vmem_limit_bytes=VMEM_LIMIT)


def _full(shape):
    n = len(shape)
    return pl.BlockSpec(shape, lambda *_: (0,) * n)


def _t5_bucket(rel):
    n = jnp.maximum(rel, 0)
    large = REL_EXACT + (jnp.log(jnp.maximum(n, 1).astype(F32) / REL_EXACT)
                         / math.log(REL_MAX_DIST / REL_EXACT) * (REL_BUCKETS - REL_EXACT)).astype(jnp.int32)
    large = jnp.minimum(large, REL_BUCKETS - 1)
    return jnp.where(n < REL_EXACT, n, large)


def _nt(a, b):
    return lax.dot_general(a, b, (((1,), (1,)), ((), ())), preferred_element_type=F32)


def _tn(a, b):
    return lax.dot_general(a, b, (((0,), (0,)), ((), ())), preferred_element_type=F32)


def _dot(a, b):
    return jnp.dot(a, b, preferred_element_type=F32)


def _project_kernel(x_ref, win_ref, qg_ref, wuq_ref, kvg_ref, wuk_ref, cos_ref, sin_ref,
                    kv_out, mlat_out, qd_out, kd_out, vd_out, vdt_out, qm_out, km_out, ct_out,
                    *, a_scale, mla_scale, q_lora, kv_lora):
    tm = x_ref.shape[0]
    x = x_ref[...].astype(BF16)
    z = _dot(x, win_ref[...])
    aw = A_WIDTH
    for j in range(2 * A_HEADS):
        kv_out[pl.ds(j, tm, stride=2 * A_HEADS), :] = z[:, aw + j * A_DV:aw + (j + 1) * A_DV]
    qd_out[...] = (z[:, :aw] * a_scale).astype(BF16)
    kd_out[...] = z[:, aw:2 * aw].astype(BF16)
    v = z[:, 2 * aw:3 * aw]
    vd_out[...] = v.astype(BF16)
    vdt_out[...] = v.T.astype(BF16)
    o = 3 * aw
    cq = z[:, o:o + q_lora]
    o += q_lora
    ckv = z[:, o:o + kv_lora]
    o += kv_lora
    kr = z[:, o:o + LANES]
    krs = z[:, o + LANES:o + 2 * LANES]
    cos = cos_ref[...]
    sin = sin_ref[...]

    cqn = cq * lax.rsqrt(jnp.mean(cq * cq, axis=-1, keepdims=True) + 1e-6) * qg_ref[...]
    q = _dot(cqn.astype(BF16), wuq_ref[...])
    ckvn = ckv * lax.rsqrt(jnp.mean(ckv * ckv, axis=-1, keepdims=True) + 1e-6) * kvg_ref[...]
    krr = kr * cos + krs * sin

    km_out[:, :kv_lora] = ckvn.astype(BF16)
    km_out[:, kv_lora:] = krr.astype(BF16)
    mla_t = jnp.concatenate([ckvn, krr], axis=1).T
    mlat_out[0] = mla_t[:kv_lora + B_DR]
    ct_out[...] = mla_t[:kv_lora].astype(BF16)

    nope_w = B_HEADS * B_DN
    rope_w = B_HEADS * LANES
    for h in range(B_HEADS):
        ql = _dot(q[:, h * B_DN:(h + 1) * B_DN].astype(BF16), wuk_ref[h])
        qr = (q[:, nope_w + h * LANES:nope_w + (h + 1) * LANES] * cos
              + q[:, nope_w + rope_w + h * LANES:nope_w + rope_w + (h + 1) * LANES] * sin)
        base = h * MLA_PAD
        qm_out[:, base:base + kv_lora] = (ql * mla_scale).astype(BF16)
        qm_out[:, base + kv_lora:base + MLA_PAD] = (qr * mla_scale).astype(BF16)


def _project(x2d, cos, sin, win, qg, wuq, kvg, wuk, *, groups, tm, q_lora, kv_lora):
    t, d = x2d.shape
    aw = A_WIDTH
    mla_row = kv_lora + B_DR
    tiles_per_group = t // groups // tm
    kern = functools.partial(_project_kernel, a_scale=A_HALF ** -0.5 * LOG2E,
                             mla_scale=(B_DN + B_DR) ** -0.5 * LOG2E, q_lora=q_lora, kv_lora=kv_lora)
    row = lambda w: pl.BlockSpec((tm, w), lambda i: (i, 0))
    col = lambda w: pl.BlockSpec((w, tm), lambda i: (0, i))
    return pl.pallas_call(
        kern,
        grid=(t // tm,),
        in_specs=[row(d), _full(win.shape), _full(qg.shape), _full(wuq.shape), _full(kvg.shape),
                  _full(wuk.shape), row(LANES), row(LANES)],
        out_specs=[pl.BlockSpec((tm * 2 * A_HEADS, A_DV), lambda i: (i, 0)),
                   pl.BlockSpec((1, mla_row, tm), lambda i: (i // tiles_per_group, 0, i % tiles_per_group)),
                   row(aw), row(aw), row(aw), col(aw), row(B_HEADS * MLA_PAD), row(MLA_PAD), col(kv_lora)],
        out_shape=[jax.ShapeDtypeStruct((t * 2 * A_HEADS, A_DV), F32),
                   jax.ShapeDtypeStruct((groups, mla_row, t // groups), F32),
                   jax.ShapeDtypeStruct((t, aw), BF16), jax.ShapeDtypeStruct((t, aw), BF16),
                   jax.ShapeDtypeStruct((t, aw), BF16), jax.ShapeDtypeStruct((aw, t), BF16),
                   jax.ShapeDtypeStruct((t, B_HEADS * MLA_PAD), BF16),
                   jax.ShapeDtypeStruct((t, MLA_PAD), BF16), jax.ShapeDtypeStruct((kv_lora, t), BF16)],
        compiler_params=_cparams(("parallel",)),
        name="project",
    )(x2d, win, qg, wuq, kvg, wuk, cos, sin)


def _lambda_value(lam_ref, lam_init):
    lp = lam_ref[...]
    l01 = jnp.sum(lp[0:1] * lp[1:2], axis=-1, keepdims=True)
    l23 = jnp.sum(lp[2:3] * lp[3:4], axis=-1, keepdims=True)
    return jnp.exp(l01) - jnp.exp(l23) + lam_init


def _update_t(s_t, pv, m_ref, l_ref, acc_ref):
    m_prev = m_ref[...]
    m_new = jnp.maximum(m_prev, jnp.max(s_t, axis=0, keepdims=True))
    alpha = jnp.exp2(m_prev - m_new)
    e = jnp.exp2(s_t - m_new)
    l_ref[...] = alpha * l_ref[...] + jnp.sum(e, axis=0, keepdims=True)
    acc_ref[...] = alpha * acc_ref[...] + pv(e.astype(BF16))
    m_ref[...] = m_new


def _prompt_attn_kernel(qi_ref, kj_ref, qd_ref, qm_ref, kd_ref, vdt_ref, km_ref, ct_ref, bias_ref, lam_ref,
                        subg_ref, wuvt_ref, o_ref, m_d, l_d, acc_d, m_m, l_m, acc_m, *, lam_init, ratio):
    p = pl.program_id(1)
    qi = qi_ref[p]
    kj = kj_ref[p]

    @pl.when(kj == 0)
    def _():
        m_d[...] = jnp.full(m_d.shape, -jnp.inf, F32)
        l_d[...] = jnp.zeros(l_d.shape, F32)
        acc_d[...] = jnp.zeros(acc_d.shape, F32)
        m_m[...] = jnp.full(m_m.shape, -jnp.inf, F32)
        l_m[...] = jnp.zeros(l_m.shape, F32)
        acc_m[...] = jnp.zeros(acc_m.shape, F32)

    bq = qd_ref.shape[0]
    lane = lax.broadcasted_iota(jnp.int32, (bq, A_DV), 1)
    parts, biases = [], []
    for h in range(A_HEADS):
        hs = slice(h * A_DV, (h + 1) * A_DV)
        qh = qd_ref[:, hs]
        zero = jnp.zeros_like(qh)
        q2 = jnp.concatenate([jnp.where(lane < A_HALF, qh, zero), jnp.where(lane >= A_HALF, qh, zero)], axis=0)
        parts.append(_nt(kd_ref[:, hs], q2))
        biases += [bias_ref[0, h]] * 2
    s_t = jnp.concatenate(parts, axis=1) + jnp.concatenate(biases, axis=1)

    def pv_d(e):
        return jnp.concatenate([_dot(vdt_ref[h * A_DV:(h + 1) * A_DV, :], e[:, 2 * h * bq:2 * (h + 1) * bq])
                                for h in range(A_HEADS)], axis=1)

    _update_t(s_t, pv_d, m_d, l_d, acc_d)

    q_rows = jnp.concatenate([qm_ref[:, h * MLA_PAD:(h + 1) * MLA_PAD] for h in range(B_HEADS)], axis=0)
    s_t = _nt(km_ref[...], q_rows) + jnp.concatenate([bias_ref[0, A_HEADS]] * B_HEADS, axis=1)
    _update_t(s_t, lambda e: _dot(ct_ref[...], e), m_m, l_m, acc_m)

    @pl.when(kj == ratio * qi + (ratio - 1))
    def _():
        lam = _lambda_value(lam_ref, lam_init)
        o_d = acc_d[...] / l_d[...]
        for h in range(A_HEADS):
            o_a = o_d[:, 2 * h * bq:(2 * h + 1) * bq]
            o_b = o_d[:, (2 * h + 1) * bq:(2 * h + 2) * bq]
            w = o_a - lam * o_b
            w = w * lax.rsqrt(jnp.mean(w * w, axis=0, keepdims=True) + 1e-5) * subg_ref[...] * (1.0 - lam_init)
            o_ref[h * A_DV:(h + 1) * A_DV, :] = w.astype(o_ref.dtype)
        o_m = (acc_m[...] / l_m[...]).astype(BF16)
        for h in range(B_HEADS):
            o = _dot(wuvt_ref[h], o_m[:, h * bq:(h + 1) * bq])
            o_ref[A_WIDTH + h * B_DV:A_WIDTH + (h + 1) * B_DV, :] = o.astype(o_ref.dtype)


def _prompt_attn(qd, qm, kd, vdt, km, ct, bias_t, lam_p, subg_col, wuvt, *, batch, seq, bq, bk, lam_init, kv_lora):
    nq, nk, ratio = seq // bq, seq // bk, bq // bk
    far = bias_t.shape[0] - 1
    pairs = [(i, j) for i in range(nq) for j in range(ratio * i + ratio)]
    qi_arr = jnp.asarray(np.array([p[0] for p in pairs], np.int32))
    kj_arr = jnp.asarray(np.array([p[1] for p in pairs], np.int32))
    mix_w = A_WIDTH + B_HEADS * B_DV
    qmap = lambda b, p, qi, kj: (b * nq + qi[p], 0)
    kmap = lambda b, p, qi, kj: (b * nk + kj[p], 0)
    kmap_t = lambda b, p, qi, kj: (0, b * nk + kj[p])
    bmap = lambda b, p, qi, kj: (jnp.minimum(ratio * qi[p] + (ratio - 1) - kj[p], far), 0, 0, 0)
    cst = lambda shape: pl.BlockSpec(shape, lambda b, p, qi, kj: (0,) * len(shape))
    grid_spec = pltpu.PrefetchScalarGridSpec(
        num_scalar_prefetch=2,
        grid=(batch, len(pairs)),
        in_specs=[pl.BlockSpec((bq, A_WIDTH), qmap), pl.BlockSpec((bq, B_HEADS * MLA_PAD), qmap),
                  pl.BlockSpec((bk, A_WIDTH), kmap), pl.BlockSpec((A_WIDTH, bk), kmap_t),
                  pl.BlockSpec((bk, MLA_PAD), kmap), pl.BlockSpec((kv_lora, bk), kmap_t),
                  pl.BlockSpec((1, A_HEADS + 1, bk, bq), bmap),
                  cst(lam_p.shape), cst(subg_col.shape), cst(wuvt.shape)],
        out_specs=pl.BlockSpec((mix_w, bq), lambda b, p, qi, kj: (0, b * nq + qi[p])),
        scratch_shapes=[pltpu.VMEM((1, 2 * A_HEADS * bq), F32), pltpu.VMEM((1, 2 * A_HEADS * bq), F32),
                        pltpu.VMEM((A_DV, 2 * A_HEADS * bq), F32),
                        pltpu.VMEM((1, B_HEADS * bq), F32), pltpu.VMEM((1, B_HEADS * bq), F32),
                        pltpu.VMEM((kv_lora, B_HEADS * bq), F32)],
    )
    return pl.pallas_call(
        functools.partial(_prompt_attn_kernel, lam_init=lam_init, ratio=ratio),
        grid_spec=grid_spec,
        out_shape=jax.ShapeDtypeStruct((mix_w, batch * seq), BF16),
        compiler_params=_cparams(("parallel", "arbitrary")),
        name="prompt_attn",
    )(qi_arr, kj_arr, qd, qm, kd, vdt, km, ct, bias_t, lam_p, subg_col, wuvt)


def _update(s, pv, m_ref, l_ref, acc_ref):
    m_prev = m_ref[...]
    m_new = jnp.maximum(m_prev, jnp.max(s, axis=-1, keepdims=True))
    alpha = jnp.exp2(m_prev - m_new)
    e = jnp.exp2(s - m_new)
    l_ref[...] = alpha * l_ref[...] + jnp.sum(e, axis=-1, keepdims=True)
    acc_ref[...] = alpha * acc_ref[...] + pv(e.astype(BF16))
    m_ref[...] = m_new


def _sample_attn_kernel(pt_ref, qd_ref, qm_ref, kdn_ref, vdn_ref, kmn_ref, bias_ref, biasn_ref, maskn_ref,
                        lam_ref, subg_ref, wuv_ref, *rest, lam_init, kv_lora, n_per_step):
    kv_refs = rest[:n_per_step]
    mla_refs = rest[n_per_step:2 * n_per_step]
    o_ref, qbd, qmm, m_d, l_d, acc_d, m_m, l_m, acc_m = rest[2 * n_per_step:]
    p = pl.program_id(1)
    dec = qd_ref.shape[1]
    n_maps = 2 * A_HEADS
    mla_row = kv_lora + B_DR
    page = mla_refs[0].shape[2]

    @pl.when(p == 0)
    def _():
        q8 = qd_ref[0].astype(F32)
        q_rep = jnp.concatenate([q8] * n_maps, axis=0)
        row = lax.broadcasted_iota(jnp.int32, q_rep.shape, 0)
        lane = lax.broadcasted_iota(jnp.int32, q_rep.shape, 1)
        qbd[...] = jnp.where(lane // A_HALF == row // dec, q_rep, 0.0).astype(BF16)
        qm8 = qm_ref[0].astype(F32)
        qmm[...] = jnp.concatenate([qm8[:, h * MLA_PAD:h * MLA_PAD + mla_row] for h in range(B_HEADS)],
                                   axis=0).astype(BF16)
        m_d[...] = jnp.full(m_d.shape, -jnp.inf, F32)
        l_d[...] = jnp.zeros(l_d.shape, F32)
        acc_d[...] = jnp.zeros(acc_d.shape, F32)
        m_m[...] = jnp.full(m_m.shape, -jnp.inf, F32)
        l_m[...] = jnp.zeros(l_m.shape, F32)
        acc_m[...] = jnp.zeros(acc_m.shape, F32)

    q_d = qbd[...]
    q_m = qmm[...]
    heads = lambda ref, off: jnp.concatenate(
        [ref[pl.ds(off + h, page, stride=2 * A_HEADS), :].astype(BF16) for h in range(A_HEADS)], axis=1)
    k_all = jnp.concatenate([heads(r, 0) for r in kv_refs], axis=0)
    v_all = jnp.concatenate([heads(r, A_HEADS) for r in kv_refs], axis=0)
    r_all = jnp.concatenate([r[0].astype(BF16) for r in mla_refs], axis=1)

    _update(_nt(q_d, k_all) + bias_ref[0], lambda e: _dot(e, v_all), m_d, l_d, acc_d)
    _update(_dot(q_m, r_all), lambda e: _nt(e, r_all[:kv_lora]), m_m, l_m, acc_m)

    @pl.when(p == pl.num_programs(1) - 1)
    def _():
        vdn = vdn_ref[0]
        _update(_nt(q_d, kdn_ref[0]) + biasn_ref[...], lambda e: _dot(e, vdn), m_d, l_d, acc_d)
        kmn = kmn_ref[0]
        _update(_nt(q_m, kmn[:, :mla_row]) + maskn_ref[...], lambda e: _dot(e, kmn[:, :kv_lora]), m_m, l_m, acc_m)

        lam = _lambda_value(lam_ref, lam_init)
        o_all = acc_d[...] / l_d[...]
        for h in range(A_HEADS):
            ra = (2 * h) * dec
            rb = (2 * h + 1) * dec
            o_a = o_all[ra:ra + dec, h * A_DV:(h + 1) * A_DV]
            o_b = o_all[rb:rb + dec, h * A_DV:(h + 1) * A_DV]
            w = o_a - lam * o_b
            w = w * lax.rsqrt(jnp.mean(w * w, axis=-1, keepdims=True) + 1e-5) * subg_ref[...] * (1.0 - lam_init)
            o_ref[0, :, h * A_DV:(h + 1) * A_DV] = w.astype(o_ref.dtype)
        ol_all = acc_m[...] / l_m[...]
        for h in range(B_HEADS):
            ol = ol_all[h * dec:(h + 1) * dec]
            o = _dot(ol.astype(BF16), wuv_ref[h])
            o_ref[0, :, A_WIDTH + h * B_DV:A_WIDTH + (h + 1) * B_DV] = o.astype(o_ref.dtype)


def _sample_attn(page_table, qd, qm, kdn, vdn, kmn, cache_kv, cache_mla_t, bias, biasn, maskn, lam_p, subg, wuv,
                 *, layer_off, lam_init, kv_lora, n_per_step):
    sb, dec, _ = qd.shape
    n_pages = page_table.shape[1]
    mla_row, page = cache_mla_t.shape[1], cache_mla_t.shape[2]
    mix_w = A_WIDTH + B_HEADS * B_DV
    steps = n_pages // n_per_step
    rows_d = 2 * A_HEADS * dec
    rows_m = B_HEADS * dec
    pt_flat = page_table.reshape(-1)

    seq3 = lambda w: pl.BlockSpec((1, dec, w), lambda s, p, pt: (s, 0, 0))
    cst = lambda shape: pl.BlockSpec(shape, lambda s, p, pt: (0,) * len(shape))

    def page_id(j):
        return lambda s, p, pt: pt[s * n_pages + p * n_per_step + j] + layer_off

    in_specs = [seq3(A_WIDTH), seq3(B_HEADS * MLA_PAD), seq3(A_WIDTH), seq3(A_WIDTH), seq3(MLA_PAD),
                pl.BlockSpec((1, rows_d, n_per_step * page),
                             lambda s, p, pt: (jnp.where(p == steps - 1, 1, 0), 0, 0)),
                cst(biasn.shape), cst(maskn.shape), cst(lam_p.shape), cst(subg.shape), cst(wuv.shape)]
    for j in range(n_per_step):
        in_specs.append(pl.BlockSpec((page * 2 * A_HEADS, A_DV), lambda s, p, pt, f=page_id(j): (f(s, p, pt), 0)))
    for j in range(n_per_step):
        in_specs.append(pl.BlockSpec((1, mla_row, page), lambda s, p, pt, f=page_id(j): (f(s, p, pt), 0, 0)))
    kern = functools.partial(_sample_attn_kernel, lam_init=lam_init, kv_lora=kv_lora, n_per_step=n_per_step)
    grid_spec = pltpu.PrefetchScalarGridSpec(
        num_scalar_prefetch=1,
        grid=(sb, steps),
        in_specs=in_specs,
        out_specs=pl.BlockSpec((1, dec, mix_w), lambda s, p, pt: (s, 0, 0)),
        scratch_shapes=[pltpu.VMEM((rows_d, A_WIDTH), BF16), pltpu.VMEM((rows_m, mla_row), BF16),
                        pltpu.VMEM((rows_d, 1), F32), pltpu.VMEM((rows_d, 1), F32),
                        pltpu.VMEM((rows_d, A_WIDTH), F32),
                        pltpu.VMEM((rows_m, 1), F32), pltpu.VMEM((rows_m, 1), F32),
                        pltpu.VMEM((rows_m, kv_lora), F32)],
    )
    return pl.pallas_call(
        kern,
        grid_spec=grid_spec,
        out_shape=jax.ShapeDtypeStruct((sb, dec, mix_w), BF16),
        compiler_params=_cparams(("parallel", "arbitrary")),
        name="sample_attn",
    )(pt_flat, qd, qm, kdn, vdn, kmn, bias, biasn, maskn, lam_p, subg, wuv,
      *([cache_kv] * n_per_step), *([cache_mla_t] * n_per_step))


def _layer_norm(x, g, b):
    xc = x - jnp.mean(x, axis=-1, keepdims=True)
    var = jnp.mean(xc * xc, axis=-1, keepdims=True)
    return xc * lax.rsqrt(var + 1e-5) * g + b


def _extract_topk(s, k_top, want_rank):
    n, t = s.shape
    iota = lax.broadcasted_iota(jnp.int32, (n, t), 0).astype(F32)
    iota_k = lax.broadcasted_iota(jnp.int32, (k_top, t), 0)
    rank = jnp.full((n, t), float(n), F32) if want_rank else None
    vals = jnp.zeros((k_top, t), F32)
    picks = []
    for k in range(k_top):
        m = jnp.max(s, axis=0, keepdims=True)
        idx = jnp.min(jnp.where(s == m, iota, float(n)), axis=0, keepdims=True)
        sel = iota == idx
        if want_rank:
            rank = jnp.where(sel, float(k), rank)
        s = jnp.where(sel, -jnp.inf, s)
        vals = jnp.where(iota_k == k, m, vals)
        picks.append(idx)
    return rank, vals, picks, s


def _post_select_kernel(mix_ref, x_ref, wout_ref, g_ref, b_ref, wq_ref, sk_ref,
                        h_out, hbt_out, a_out, cnt_out, b_out, r2_out, *, alpha, mix_transposed):
    a = _tn(mix_ref[...], wout_ref[...]) if mix_transposed else _dot(mix_ref[...], wout_ref[...])
    h = _layer_norm(alpha * x_ref[...] + a, g_ref[...], b_ref[...])
    h_out[...] = h
    hb = h.astype(BF16)
    hbt_out[...] = h.T.astype(BF16)
    q = _dot(hb, wq_ref[...])
    half = PEER_DK // 2
    kk = PEER_TOPK
    sub = kk // 2
    iota_n = lax.broadcasted_iota(jnp.int32, (N_KEYS, q.shape[0]), 0).astype(F32)
    for hd in range(PEER_HEADS):
        q1 = q[:, hd * PEER_DK:hd * PEER_DK + half].astype(BF16)
        q2 = q[:, hd * PEER_DK + half:(hd + 1) * PEER_DK].astype(BF16)
        s1 = _nt(sk_ref[0], q1)
        s2 = _nt(sk_ref[1], q2)
        _, v1, picks1, s1_left = _extract_topk(s1, kk, False)
        r2, v2, _, _ = _extract_topk(s2, kk, True)
        cand = jnp.concatenate([v1[0:1] + v2] + [v1[k:k + 1] + v2[:sub] for k in range(1, kk)], axis=0)
        _, _, _, cand_left = _extract_topk(cand, kk, False)
        picked = cand_left == -jnp.inf
        e = jnp.where(picked, jnp.exp(cand - (v1[0:1] + v2[0:1])), 0.0)
        z = jnp.sum(e, axis=0, keepdims=True)
        pf = picked.astype(F32)
        cnt = jnp.zeros_like(s1)
        for k in range(kk):
            rows = pf[0:kk] if k == 0 else pf[kk + (k - 1) * sub:kk + k * sub]
            c_k = jnp.sum(rows, axis=0, keepdims=True)
            cnt = jnp.where(iota_n == picks1[k], c_k, cnt)
        a_out[hd] = jnp.where(s1_left == -jnp.inf, jnp.exp(s1 - v1[0:1]), 0.0) / z
        cnt_out[hd] = cnt
        b_out[hd] = jnp.where(r2 < float(kk), jnp.exp(s2 - v2[0:1]), 0.0).astype(b_out.dtype)
        r2_out[hd] = r2.astype(r2_out.dtype)


def _post_select(mix, x2d, wout, g, b, wq, sk, *, tb, alpha, mix_transposed):
    t, d = x2d.shape
    row = lambda w: pl.BlockSpec((tb, w), lambda i: (i, 0))
    mix_spec = pl.BlockSpec((mix.shape[0], tb), lambda i: (0, i)) if mix_transposed else row(mix.shape[1])
    sel = pl.BlockSpec((PEER_HEADS, N_KEYS, tb), lambda i: (0, 0, i))
    sel_shape = jax.ShapeDtypeStruct((PEER_HEADS, N_KEYS, t), F32)
    sel_half = jax.ShapeDtypeStruct((PEER_HEADS, N_KEYS, t), BF16)
    return pl.pallas_call(
        functools.partial(_post_select_kernel, alpha=alpha, mix_transposed=mix_transposed),
        grid=(t // tb,),
        in_specs=[mix_spec, row(d), _full(wout.shape), _full(g.shape), _full(b.shape),
                  _full(wq.shape), _full(sk.shape)],
        out_specs=[row(d), pl.BlockSpec((d, tb), lambda i: (0, i)), sel, sel, sel, sel],
        out_shape=[jax.ShapeDtypeStruct((t, d), F32), jax.ShapeDtypeStruct((d, t), BF16),
                   sel_shape, sel_shape, sel_half, sel_half],
        compiler_params=_cparams(("parallel",)),
        name="post_select",
    )(mix, x2d, wout, g, b, wq, sk)


def _peer_dense_kernel(h_ref, hbt_ref, a_ref, cnt_ref, b_ref, r2_ref, u_ref, v_ref, g_ref, bb_ref,
                       o_ref, acc, s_even, s_odd, w_scr, *, alpha, rows_per_step):
    e = pl.program_id(1)

    @pl.when(e == 0)
    def _():
        acc[...] = jnp.zeros(acc.shape, F32)

    tb = hbt_ref.shape[1]
    pack = 2 * SUBLANES
    spread = lambda row: jnp.concatenate([jnp.broadcast_to(row, (pack, tb)).astype(BF16)] * (N_KEYS // pack), axis=0)
    pair_rows = 2 * N_KEYS
    n_pairs = rows_per_step // 2

    def scores(pair):
        rows = pl.ds(pl.multiple_of(pair * pair_rows, pair_rows), pair_rows)
        return _dot(u_ref[rows, :], hbt_ref[...])

    def weights(pair, s_ref):
        for half in range(2):
            r = 2 * pair + half
            i = e * rows_per_step + r
            s = s_ref[half * N_KEYS:(half + 1) * N_KEYS, :]
            act = (0.5 * s * (1.0 + lax.erf(s * (2.0 ** -0.5)))).astype(BF16)
            zero = jnp.zeros((N_KEYS, tb), BF16)
            gate = zero
            for hd in range(PEER_HEADS):
                a_b = spread(a_ref[hd, pl.ds(i, 1), :])
                c_b = spread(cnt_ref[hd, pl.ds(i, 1), :])
                gate = gate + jnp.where(r2_ref[hd] < c_b, b_ref[hd], zero) * a_b
            w_scr[pl.ds(pl.multiple_of(r * N_KEYS, N_KEYS), N_KEYS), :] = gate * act

    s_even[...] = scores(0)

    def two_pairs(k, carry):
        s_odd[...] = scores(2 * k + 1)
        weights(2 * k, s_even)
        s_even[...] = scores(jnp.minimum(2 * k + 2, n_pairs - 1))
        weights(2 * k + 1, s_odd)
        return carry

    lax.fori_loop(0, n_pairs // 2, two_pairs, 0)
    acc[...] += _tn(w_scr[...], v_ref[...])

    @pl.when(e == pl.num_programs(1) - 1)
    def _():
        o_ref[...] = _layer_norm(alpha * h_ref[...] + acc[...], g_ref[...], bb_ref[...])


def _peer_dense(h, hbt, sel_a, sel_cnt, sel_b, sel_r2, u_tab, v_tab, g, b, *, tb, rows_per_step, alpha):
    t, d = h.shape
    n_exp = u_tab.shape[0]
    eb = rows_per_step * N_KEYS
    row = lambda w: pl.BlockSpec((tb, w), lambda i, e: (i, 0))
    sel = pl.BlockSpec((PEER_HEADS, N_KEYS, tb), lambda i, e: (0, 0, i))
    tab = pl.BlockSpec((eb, d), lambda i, e: (e, 0))
    cst = lambda shape: pl.BlockSpec(shape, lambda i, e: (0,) * len(shape))
    return pl.pallas_call(
        functools.partial(_peer_dense_kernel, alpha=alpha, rows_per_step=rows_per_step),
        grid=(t // tb, n_exp // eb),
        in_specs=[row(d), pl.BlockSpec((d, tb), lambda i, e: (0, i)), sel, sel, sel, sel, tab, tab,
                  cst(g.shape), cst(b.shape)],
        out_specs=row(d),
        out_shape=jax.ShapeDtypeStruct((t, d), F32),
        scratch_shapes=[pltpu.VMEM((tb, d), F32), pltpu.VMEM((2 * N_KEYS, tb), F32),
                        pltpu.VMEM((2 * N_KEYS, tb), F32), pltpu.VMEM((eb, tb), BF16)],
        compiler_params=_cparams(("parallel", "arbitrary")),
        name="peer_dense",
    )(h, hbt, sel_a, sel_cnt, sel_b, sel_r2, u_tab, v_tab, g, b)


def _rope_tables(pos):
    inv = 1.0 / (ROPE_THETA ** (jnp.arange(0, B_DR, 2, dtype=F32) / B_DR))
    ang = pos.astype(F32)[:, None] * inv[None, :]
    cos, sin = jnp.cos(ang), jnp.sin(ang)
    pad = jnp.zeros((pos.shape[0], LANES - B_DR), F32)
    return (jnp.concatenate([cos, cos, pad], axis=-1), jnp.concatenate([-sin, sin, pad], axis=-1))


def _swap_halves(w):
    half = w.shape[-1] // 2
    return jnp.concatenate([w[..., half:], w[..., :half]], axis=-1)


def _pad_lanes(w):
    return jnp.pad(w, [(0, 0)] * (w.ndim - 1) + [(0, LANES - w.shape[-1])])


def _pick(n, prefs):
    for c in prefs:
        if n % c == 0:
            return c
    return n


def _rel_bias_values(rel_bias, rel):
    onehot = (_t5_bucket(rel)[..., None] == jnp.arange(REL_BUCKETS)).astype(F32)
    vals = jnp.einsum('...b,bh->h...', onehot, rel_bias.astype(F32), precision=lax.Precision.HIGHEST)
    return vals * LOG2E


def kernel(x_prompt, x_sample, cache_diff_kv, cache_mla, page_table, w_in, diff_lambda, diff_subln_g, rel_bias,
           mla_q_norm_g, mla_w_uq, mla_kv_norm_g, mla_w_uk, mla_w_uv, w_out, ln1_g, ln1_b, peer_w_q,
           peer_sub_keys, peer_u, peer_v, ln2_g, ln2_b):
    batch, seq, d = x_prompt.shape
    sb, dec, _ = x_sample.shape
    depth = w_in.shape[0]
    n_pool, page = cache_mla.shape[1], cache_mla.shape[2]
    n_pages = page_table.shape[1]
    past = n_pages * page
    q_lora = mla_q_norm_g.shape[1]
    kv_lora = mla_kv_norm_g.shape[1]
    mla_row = kv_lora + B_DR
    alpha = (2 * depth) ** 0.25
    aw = A_WIDTH

    pos_p = jnp.arange(seq, dtype=jnp.int32)
    pos_s = past + jnp.arange(dec, dtype=jnp.int32)
    cos_p, sin_p = _rope_tables(jnp.tile(pos_p, batch))
    cos_s, sin_s = _rope_tables(jnp.tile(pos_s, sb))

    rel_sat = REL_EXACT * (REL_MAX_DIST / REL_EXACT) ** ((REL_BUCKETS - 1 - REL_EXACT) / (REL_BUCKETS - REL_EXACT))
    bq = _pick(seq, (512, 256, 128))
    bk = _pick(seq, (256, 128))
    ratio = bq // bk
    assert rel_sat < page

    rel_t = jnp.arange(bq, dtype=jnp.int32)[None, :] - jnp.arange(bk, dtype=jnp.int32)[:, None]
    tiles = []
    t = 0
    while True:
        off = (t - (ratio - 1)) * bk
        rel = rel_t + off
        ok = rel >= 0
        tiles.append(jnp.concatenate([jnp.where(ok[None], _rel_bias_values(rel_bias, rel), NEG),
                                      jnp.where(ok, 0.0, NEG)[None]], axis=0))
        if off - (bk - 1) >= rel_sat:
            break
        t += 1
    bias_p = jnp.stack(tiles)

    n_per_step = _pick(n_pages, (16, 8, 4, 2, 1))
    rows_d = 2 * A_HEADS * dec
    rep = lambda v: jnp.broadcast_to(v[:, None], (A_HEADS, 2) + v.shape[1:]).reshape((rows_d,) + v.shape[2:])
    far_s = jnp.broadcast_to(_rel_bias_values(rel_bias, jnp.full((1, 1), past, jnp.int32)),
                             (A_HEADS, dec, n_per_step * page))
    kpos_last = past - page + jnp.arange(page, dtype=jnp.int32)
    last = _rel_bias_values(rel_bias, pos_s[:, None] - kpos_last[None, :])
    bias_s = jnp.stack([rep(far_s), rep(jnp.concatenate([far_s[..., :(n_per_step - 1) * page], last], axis=-1))])
    rd = jnp.arange(dec, dtype=jnp.int32)
    rel_n = rd[:, None] - rd[None, :]
    bias_n = rep(jnp.where(rel_n[None] >= 0, _rel_bias_values(rel_bias, rel_n), NEG))
    mask_n = jnp.tile(jnp.where(rel_n >= 0, 0.0, NEG).astype(F32), (B_HEADS, 1))

    cache_kv2 = cache_diff_kv.reshape(depth * n_pool * page * 2 * A_HEADS, A_DV)
    cache_mla_t = jnp.swapaxes(cache_mla, 2, 3).reshape(depth * n_pool, mla_row, page)

    tm = _pick(seq, (512, 256, 128))
    tm_s = _pick(sb * dec, (512, 256, 128))
    tb = _pick(batch * seq, (512, 256, 128))
    tb_s = _pick(sb * dec, (512, 256, 128))

    hp = x_prompt.reshape(batch * seq, d)
    hs = x_sample.reshape(sb * dec, d)
    kv_p, mla_p, kv_s, mla_s = [], [], [], []
    for l in range(depth):
        lam_init = 0.8 - 0.6 * math.exp(-0.3 * l)
        w = w_in[l]
        o = 3 * aw + q_lora + kv_lora
        kr_w = w[:, o:]
        win = jnp.concatenate([w[:, :o], _pad_lanes(kr_w), _pad_lanes(_swap_halves(kr_w))], axis=-1).astype(BF16)
        uq = mla_w_uq[l]
        uq_r = uq[..., B_DN:]
        wuq = jnp.concatenate([uq[..., :B_DN].reshape(q_lora, -1), _pad_lanes(uq_r).reshape(q_lora, -1),
                               _pad_lanes(_swap_halves(uq_r)).reshape(q_lora, -1)], axis=-1).astype(BF16)
        wuk = jnp.transpose(mla_w_uk[l], (1, 2, 0)).astype(BF16)
        wuv = jnp.transpose(mla_w_uv[l], (1, 0, 2)).astype(BF16)
        wuvt = jnp.transpose(mla_w_uv[l], (1, 2, 0)).astype(BF16)
        qg = mla_q_norm_g[l].reshape(1, -1)
        kvg = mla_kv_norm_g[l].reshape(1, -1)
        subg = diff_subln_g[l].reshape(1, -1)
        subg_col = diff_subln_g[l].reshape(-1, 1)
        lam_p = diff_lambda[l]
        wout = w_out[l].astype(BF16)
        wq = peer_w_q[l].astype(BF16)
        sk = peer_sub_keys[l].astype(BF16)
        u_tab = peer_u[l].astype(BF16)
        v_tab = peer_v[l].astype(BF16)
        g1, b1 = ln1_g[l].reshape(1, -1), ln1_b[l].reshape(1, -1)
        g2, b2 = ln2_g[l].reshape(1, -1), ln2_b[l].reshape(1, -1)

        def post(h2d, mix, tb_, mix_transposed):
            h1, h1b, sa, sc, sbb, sr = _post_select(mix, h2d, wout, g1, b1, wq, sk, tb=tb_, alpha=alpha,
                                                    mix_transposed=mix_transposed)
            return _peer_dense(h1, h1b, sa, sc, sbb, sr, u_tab, v_tab, g2, b2, tb=tb_, rows_per_step=8,
                               alpha=alpha)

        kv_rows, mla_t, qd, kd, _, vdt, qm, km, ct = _project(
            hp, cos_p, sin_p, win, qg, wuq, kvg, wuk, groups=batch, tm=tm, q_lora=q_lora, kv_lora=kv_lora)
        mix_t = _prompt_attn(qd, qm, kd, vdt, km, ct, bias_p, lam_p, subg_col, wuvt, batch=batch, seq=seq,
                             bq=bq, bk=bk, lam_init=lam_init, kv_lora=kv_lora)
        kv_p.append(kv_rows.reshape(batch, seq, 2, A_HEADS, A_DV))
        mla_p.append(jnp.swapaxes(mla_t, 1, 2))
        hp = post(hp, mix_t, tb, True)

        kv_rows, mla_t, qd, kd, vd, _, qm, km, _ = _project(
            hs, cos_s, sin_s, win, qg, wuq, kvg, wuk, groups=1, tm=tm_s, q_lora=q_lora, kv_lora=kv_lora)
        r3 = lambda a: a.reshape(sb, dec, a.shape[-1])
        mix = _sample_attn(page_table, r3(qd), r3(qm), r3(kd), r3(vd), r3(km), cache_kv2, cache_mla_t,
                           bias_s, bias_n, mask_n, lam_p, subg, wuv, layer_off=l * n_pool, lam_init=lam_init,
                           kv_lora=kv_lora, n_per_step=n_per_step)
        kv_s.append(kv_rows.reshape(sb, dec, 2, A_HEADS, A_DV))
        mla_s.append(jnp.swapaxes(mla_t, 1, 2).reshape(sb, dec, mla_row))
        hs = post(hs, mix.reshape(sb * dec, -1), tb_s, False)

    return (hp.reshape(batch, seq, d), hs.reshape(sb, dec, d), jnp.stack(kv_p), jnp.stack(mla_p),
            jnp.stack(kv_s), jnp.stack(mla_s))
```

```python
import functools
import math

import numpy as np
import jax
import jax.numpy as jnp
from jax import lax
from jax.experimental import pallas as pl
from jax.experimental.pallas import tpu as pltpu

F32 = jnp.float32
BF16 = jnp.bfloat16

A_HEADS = 4
A_HALF = 64
A_DV = 128
A_WIDTH = A_HEADS * A_DV
B_HEADS = 4
B_DV = 128
B_DN = 128
B_DR = 64
ROPE_THETA = 10000.0
REL_BUCKETS = 32
REL_MAX_DIST = 128
REL_EXACT = REL_BUCKETS // 2
PEER_HEADS = 8
PEER_DK = 256
N_KEYS = 128
PEER_TOPK = 16
NEG = -1e30
LOG2E = math.log2(math.e)

SUBLANES = 8
LANES = 128
MLA_PAD = 384
VMEM_LIMIT = 48 * 1024 * 1024


def _cparams(sem):
    return pltpu.CompilerParams(dimension_semantics=sem, vmem_limit_bytes=VMEM_LIMIT)


def _full(shape):
    n = len(shape)
    return pl.BlockSpec(shape, lambda *_: (0,) * n)


def _t5_bucket(rel):
    n = jnp.maximum(rel, 0)
    large = REL_EXACT + (jnp.log(jnp.maximum(n, 1).astype(F32) / REL_EXACT)
                         / math.log(REL_MAX_DIST / REL_EXACT) * (REL_BUCKETS - REL_EXACT)).astype(jnp.int32)
    large = jnp.minimum(large, REL_BUCKETS - 1)
    return jnp.where(n < REL_EXACT, n, large)


def _nt(a, b):
    return lax.dot_general(a, b, (((1,), (1,)), ((), ())), preferred_element_type=F32)


def _tn(a, b):
    return lax.dot_general(a, b, (((0,), (0,)), ((), ())), preferred_element_type=F32)


def _dot(a, b):
    return jnp.dot(a, b, preferred_element_type=F32)


def _project_kernel(x_ref, win_ref, qg_ref, wuq_ref, kvg_ref, wuk_ref, cos_ref, sin_ref,
                    kv_out, mlat_out, qd_out, kd_out, vd_out, vdt_out, qm_out, km_out, ct_out,
                    *, a_scale, mla_scale, q_lora, kv_lora):
    tm = x_ref.shape[0]
    x = x_ref[...].astype(BF16)
    z = _dot(x, win_ref[...])
    aw = A_WIDTH
    for j in range(2 * A_HEADS):
        kv_out[pl.ds(j, tm, stride=2 * A_HEADS), :] = z[:, aw + j * A_DV:aw + (j + 1) * A_DV]
    qd_out[...] = (z[:, :aw] * a_scale).astype(BF16)
    kd_out[...] = z[:, aw:2 * aw].astype(BF16)
    v = z[:, 2 * aw:3 * aw]
    vd_out[...] = v.astype(BF16)
    vdt_out[...] = v.T.astype(BF16)
    o = 3 * aw
    cq = z[:, o:o + q_lora]
    o += q_lora
    ckv = z[:, o:o + kv_lora]
    o += kv_lora
    kr = z[:, o:o + LANES]
    krs = z[:, o + LANES:o + 2 * LANES]
    cos = cos_ref[...]
    sin = sin_ref[...]

    cqn = cq * lax.rsqrt(jnp.mean(cq * cq, axis=-1, keepdims=True) + 1e-6) * qg_ref[...]
    q = _dot(cqn.astype(BF16), wuq_ref[...])
    ckvn = ckv * lax.rsqrt(jnp.mean(ckv * ckv, axis=-1, keepdims=True) + 1e-6) * kvg_ref[...]
    krr = kr * cos + krs * sin

    km_out[:, :kv_lora] = ckvn.astype(BF16)
    km_out[:, kv_lora:] = krr.astype(BF16)
    mla_t = jnp.concatenate([ckvn, krr], axis=1).T
    mlat_out[0] = mla_t[:kv_lora + B_DR]
    ct_out[...] = mla_t[:kv_lora].astype(BF16)

    nope_w = B_HEADS * B_DN
    rope_w = B_HEADS * LANES
    for h in range(B_HEADS):
        ql = _dot(q[:, h * B_DN:(h + 1) * B_DN].astype(BF16), wuk_ref[h])
        qr = (q[:, nope_w + h * LANES:nope_w + (h + 1) * LANES] * cos
              + q[:, nope_w + rope_w + h * LANES:nope_w + rope_w + (h + 1) * LANES] * sin)
        base = h * MLA_PAD
        qm_out[:, base:base + kv_lora] = (ql * mla_scale).astype(BF16)
        qm_out[:, base + kv_lora:base + MLA_PAD] = (qr * mla_scale).astype(BF16)


def _project(x2d, cos, sin, win, qg, wuq, kvg, wuk, *, groups, tm, q_lora, kv_lora):
    t, d = x2d.shape
    aw = A_WIDTH
    mla_row = kv_lora + B_DR
    tiles_per_group = t // groups // tm
    kern = functools.partial(_project_kernel, a_scale=A_HALF ** -0.5 * LOG2E,
                             mla_scale=(B_DN + B_DR) ** -0.5 * LOG2E, q_lora=q_lora, kv_lora=kv_lora)
    row = lambda w: pl.BlockSpec((tm, w), lambda i: (i, 0))
    col = lambda w: pl.BlockSpec((w, tm), lambda i: (0, i))
    return pl.pallas_call(
        kern,
        grid=(t // tm,),
        in_specs=[row(d), _full(win.shape), _full(qg.shape), _full(wuq.shape), _full(kvg.shape),
                  _full(wuk.shape), row(LANES), row(LANES)],
        out_specs=[pl.BlockSpec((tm * 2 * A_HEADS, A_DV), lambda i: (i, 0)),
                   pl.BlockSpec((1, mla_row, tm), lambda i: (i // tiles_per_group, 0, i % tiles_per_group)),
                   row(aw), row(aw), row(aw), col(aw), row(B_HEADS * MLA_PAD), row(MLA_PAD), col(kv_lora)],
        out_shape=[jax.ShapeDtypeStruct((t * 2 * A_HEADS, A_DV), F32),
                   jax.ShapeDtypeStruct((groups, mla_row, t // groups), F32),
                   jax.ShapeDtypeStruct((t, aw), BF16), jax.ShapeDtypeStruct((t, aw), BF16),
                   jax.ShapeDtypeStruct((t, aw), BF16), jax.ShapeDtypeStruct((aw, t), BF16),
                   jax.ShapeDtypeStruct((t, B_HEADS * MLA_PAD), BF16),
                   jax.ShapeDtypeStruct((t, MLA_PAD), BF16), jax.ShapeDtypeStruct((kv_lora, t), BF16)],
        compiler_params=_cparams(("parallel",)),
        name="project",
    )(x2d, win, qg, wuq, kvg, wuk, cos, sin)


def _lambda_value(lam_ref, lam_init):
    lp = lam_ref[...]
    l01 = jnp.sum(lp[0:1] * lp[1:2], axis=-1, keepdims=True)
    l23 = jnp.sum(lp[2:3] * lp[3:4], axis=-1, keepdims=True)
    return jnp.exp(l01) - jnp.exp(l23) + lam_init


def _update_t(s_t, pv, m_ref, l_ref, acc_ref):
    m_prev = m_ref[...]
    m_new = jnp.maximum(m_prev, jnp.max(s_t, axis=0, keepdims=True))
    alpha = jnp.exp2(m_prev - m_new)
    e = jnp.exp2(s_t - m_new)
    l_ref[...] = alpha * l_ref[...] + jnp.sum(e, axis=0, keepdims=True)
    acc_ref[...] = alpha * acc_ref[...] + pv(e.astype(BF16))
    m_ref[...] = m_new


def _prompt_attn_kernel(qi_ref, kj_ref, qd_ref, qm_ref, kd_ref, vdt_ref, km_ref, ct_ref, bias_ref, lam_ref,
                        subg_ref, wuvt_ref, o_ref, m_d, l_d, acc_d, m_m, l_m, acc_m, *, lam_init, ratio):
    p = pl.program_id(1)
    qi = qi_ref[p]
    kj = kj_ref[p]

    @pl.when(kj == 0)
    def _():
        m_d[...] = jnp.full(m_d.shape, -jnp.inf, F32)
        l_d[...] = jnp.zeros(l_d.shape, F32)
        acc_d[...] = jnp.zeros(acc_d.shape, F32)
        m_m[...] = jnp.full(m_m.shape, -jnp.inf, F32)
        l_m[...] = jnp.zeros(l_m.shape, F32)
        acc_m[...] = jnp.zeros(acc_m.shape, F32)

    bq = qd_ref.shape[0]
    lane = lax.broadcasted_iota(jnp.int32, (bq, A_DV), 1)
    parts, biases = [], []
    for h in range(A_HEADS):
        hs = slice(h * A_DV, (h + 1) * A_DV)
        qh = qd_ref[:, hs]
        zero = jnp.zeros_like(qh)
        q2 = jnp.concatenate([jnp.where(lane < A_HALF, qh, zero), jnp.where(lane >= A_HALF, qh, zero)], axis=0)
        parts.append(_nt(kd_ref[:, hs], q2))
        biases += [bias_ref[0, h]] * 2
    s_t = jnp.concatenate(parts, axis=1) + jnp.concatenate(biases, axis=1)

    def pv_d(e):
        return jnp.concatenate([_dot(vdt_ref[h * A_DV:(h + 1) * A_DV, :], e[:, 2 * h * bq:2 * (h + 1) * bq])
                                for h in range(A_HEADS)], axis=1)

    _update_t(s_t, pv_d, m_d, l_d, acc_d)

    q_rows = jnp.concatenate([qm_ref[:, h * MLA_PAD:(h + 1) * MLA_PAD] for h in range(B_HEADS)], axis=0)
    s_t = _nt(km_ref[...], q_rows) + jnp.concatenate([bias_ref[0, A_HEADS]] * B_HEADS, axis=1)
    _update_t(s_t, lambda e: _dot(ct_ref[...], e), m_m, l_m, acc_m)

    @pl.when(kj == ratio * qi + (ratio - 1))
    def _():
        lam = _lambda_value(lam_ref, lam_init)
        o_d = acc_d[...] / l_d[...]
        for h in range(A_HEADS):
            o_a = o_d[:, 2 * h * bq:(2 * h + 1) * bq]
            o_b = o_d[:, (2 * h + 1) * bq:(2 * h + 2) * bq]
            w = o_a - lam * o_b
            w = w * lax.rsqrt(jnp.mean(w * w, axis=0, keepdims=True) + 1e-5) * subg_ref[...] * (1.0 - lam_init)
            o_ref[h * A_DV:(h + 1) * A_DV, :] = w.astype(o_ref.dtype)
        o_m = (acc_m[...] / l_m[...]).astype(BF16)
        for h in range(B_HEADS):
            o = _dot(wuvt_ref[h], o_m[:, h * bq:(h + 1) * bq])
            o_ref[A_WIDTH + h * B_DV:A_WIDTH + (h + 1) * B_DV, :] = o.astype(o_ref.dtype)


def _prompt_attn(qd, qm, kd, vdt, km, ct, bias_t, lam_p, subg_col, wuvt, *, batch, seq, bq, bk, lam_init, kv_lora):
    nq, nk, ratio = seq // bq, seq // bk, bq // bk
    far = bias_t.shape[0] - 1
    pairs = [(i, j) for i in range(nq) for j in range(ratio * i + ratio)]
    qi_arr = jnp.asarray(np.array([p[0] for p in pairs], np.int32))
    kj_arr = jnp.asarray(np.array([p[1] for p in pairs], np.int32))
    mix_w = A_WIDTH + B_HEADS * B_DV
    qmap = lambda b, p, qi, kj: (b * nq + qi[p], 0)
    kmap = lambda b, p, qi, kj: (b * nk + kj[p], 0)
    kmap_t = lambda b, p, qi, kj: (0, b * nk + kj[p])
    bmap = lambda b, p, qi, kj: (jnp.minimum(ratio * qi[p] + (ratio - 1) - kj[p], far), 0, 0, 0)
    cst = lambda shape: pl.BlockSpec(shape, lambda b, p, qi, kj: (0,) * len(shape))
    grid_spec = pltpu.PrefetchScalarGridSpec(
        num_scalar_prefetch=2,
        grid=(batch, len(pairs)),
        in_specs=[pl.BlockSpec((bq, A_WIDTH), qmap), pl.BlockSpec((bq, B_HEADS * MLA_PAD), qmap),
                  pl.BlockSpec((bk, A_WIDTH), kmap), pl.BlockSpec((A_WIDTH, bk), kmap_t),
                  pl.BlockSpec((bk, MLA_PAD), kmap), pl.BlockSpec((kv_lora, bk), kmap_t),
                  pl.BlockSpec((1, A_HEADS + 1, bk, bq), bmap),
                  cst(lam_p.shape), cst(subg_col.shape), cst(wuvt.shape)],
        out_specs=pl.BlockSpec((mix_w, bq), lambda b, p, qi, kj: (0, b * nq + qi[p])),
        scratch_shapes=[pltpu.VMEM((1, 2 * A_HEADS * bq), F32), pltpu.VMEM((1, 2 * A_HEADS * bq), F32),
                        pltpu.VMEM((A_DV, 2 * A_HEADS * bq), F32),
                        pltpu.VMEM((1, B_HEADS * bq), F32), pltpu.VMEM((1, B_HEADS * bq), F32),
                        pltpu.VMEM((kv_lora, B_HEADS * bq), F32)],
    )
    return pl.pallas_call(
        functools.partial(_prompt_attn_kernel, lam_init=lam_init, ratio=ratio),
        grid_spec=grid_spec,
        out_shape=jax.ShapeDtypeStruct((mix_w, batch * seq), BF16),
        compiler_params=_cparams(("parallel", "arbitrary")),
        name="prompt_attn",
    )(qi_arr, kj_arr, qd, qm, kd, vdt, km, ct, bias_t, lam_p, subg_col, wuvt)


def _update(s, pv, m_ref, l_ref, acc_ref):
    m_prev = m_ref[...]
    m_new = jnp.maximum(m_prev, jnp.max(s, axis=-1, keepdims=True))
    alpha = jnp.exp2(m_prev - m_new)
    e = jnp.exp2(s - m_new)
    l_ref[...] = alpha * l_ref[...] + jnp.sum(e, axis=-1, keepdims=True)
    acc_ref[...] = alpha * acc_ref[...] + pv(e.astype(BF16))
    m_ref[...] = m_new


def _sample_attn_kernel(pt_ref, qd_ref, qm_ref, kdn_ref, vdn_ref, kmn_ref, bias_ref, biasn_ref, maskn_ref,
                        lam_ref, subg_ref, wuv_ref, *rest, lam_init, kv_lora, n_per_step):
    kv_refs = rest[:n_per_step]
    mla_refs = rest[n_per_step:2 * n_per_step]
    o_ref, qbd, qmm, m_d, l_d, acc_d, m_m, l_m, acc_m = rest[2 * n_per_step:]
    p = pl.program_id(1)
    dec = qd_ref.shape[1]
    n_maps = 2 * A_HEADS
    mla_row = kv_lora + B_DR
    page = mla_refs[0].shape[2]

    @pl.when(p == 0)
    def _():
        q8 = qd_ref[0].astype(F32)
        q_rep = jnp.concatenate([q8] * n_maps, axis=0)
        row = lax.broadcasted_iota(jnp.int32, q_rep.shape, 0)
        lane = lax.broadcasted_iota(jnp.int32, q_rep.shape, 1)
        qbd[...] = jnp.where(lane // A_HALF == row // dec, q_rep, 0.0).astype(BF16)
        qm8 = qm_ref[0].astype(F32)
        qmm[...] = jnp.concatenate([qm8[:, h * MLA_PAD:h * MLA_PAD + mla_row] for h in range(B_HEADS)],
                                   axis=0).astype(BF16)
        m_d[...] = jnp.full(m_d.shape, -jnp.inf, F32)
        l_d[...] = jnp.zeros(l_d.shape, F32)
        acc_d[...] = jnp.zeros(acc_d.shape, F32)
        m_m[...] = jnp.full(m_m.shape, -jnp.inf, F32)
        l_m[...] = jnp.zeros(l_m.shape, F32)
        acc_m[...] = jnp.zeros(acc_m.shape, F32)

    q_d = qbd[...]
    q_m = qmm[...]
    heads = lambda ref, off: jnp.concatenate(
        [ref[pl.ds(off + h, page, stride=2 * A_HEADS), :].astype(BF16) for h in range(A_HEADS)], axis=1)
    k_all = jnp.concatenate([heads(r, 0) for r in kv_refs], axis=0)
    v_all = jnp.concatenate([heads(r, A_HEADS) for r in kv_refs], axis=0)
    r_all = jnp.concatenate([r[0].astype(BF16) for r in mla_refs], axis=1)

    _update(_nt(q_d, k_all) + bias_ref[0], lambda e: _dot(e, v_all), m_d, l_d, acc_d)
    _update(_dot(q_m, r_all), lambda e: _nt(e, r_all[:kv_lora]), m_m, l_m, acc_m)

    @pl.when(p == pl.num_programs(1) - 1)
    def _():
        vdn = vdn_ref[0]
        _update(_nt(q_d, kdn_ref[0]) + biasn_ref[...], lambda e: _dot(e, vdn), m_d, l_d, acc_d)
        kmn = kmn_ref[0]
        _update(_nt(q_m, kmn[:, :mla_row]) + maskn_ref[...], lambda e: _dot(e, kmn[:, :kv_lora]), m_m, l_m, acc_m)

        lam = _lambda_value(lam_ref, lam_init)
        o_all = acc_d[...] / l_d[...]
        for h in range(A_HEADS):
            ra = (2 * h) * dec
            rb = (2 * h + 1) * dec
            o_a = o_all[ra:ra + dec, h * A_DV:(h + 1) * A_DV]
            o_b = o_all[rb:rb + dec, h * A_DV:(h + 1) * A_DV]
            w = o_a - lam * o_b
            w = w * lax.rsqrt(jnp.mean(w * w, axis=-1, keepdims=True) + 1e-5) * subg_ref[...] * (1.0 - lam_init)
            o_ref[0, :, h * A_DV:(h + 1) * A_DV] = w.astype(o_ref.dtype)
        ol_all = acc_m[...] / l_m[...]
        for h in range(B_HEADS):
            ol = ol_all[h * dec:(h + 1) * dec]
            o = _dot(ol.astype(BF16), wuv_ref[h])
            o_ref[0, :, A_WIDTH + h * B_DV:A_WIDTH + (h + 1) * B_DV] = o.astype(o_ref.dtype)


def _sample_attn(page_table, qd, qm, kdn, vdn, kmn, cache_kv, cache_mla_t, bias, biasn, maskn, lam_p, subg, wuv,
                 *, layer_off, lam_init, kv_lora, n_per_step):
    sb, dec, _ = qd.shape
    n_pages = page_table.shape[1]
    mla_row, page = cache_mla_t.shape[1], cache_mla_t.shape[2]
    mix_w = A_WIDTH + B_HEADS * B_DV
    steps = n_pages // n_per_step
    rows_d = 2 * A_HEADS * dec
    rows_m = B_HEADS * dec
    pt_flat = page_table.reshape(-1)

    seq3 = lambda w: pl.BlockSpec((1, dec, w), lambda s, p, pt: (s, 0, 0))
    cst = lambda shape: pl.BlockSpec(shape, lambda s, p, pt: (0,) * len(shape))

    def page_id(j):
        return lambda s, p, pt: pt[s * n_pages + p * n_per_step + j] + layer_off

    in_specs = [seq3(A_WIDTH), seq3(B_HEADS * MLA_PAD), seq3(A_WIDTH), seq3(A_WIDTH), seq3(MLA_PAD),
                pl.BlockSpec((1, rows_d, n_per_step * page),
                             lambda s, p, pt: (jnp.where(p == steps - 1, 1, 0), 0, 0)),
                cst(biasn.shape), cst(maskn.shape), cst(lam_p.shape), cst(subg.shape), cst(wuv.shape)]
    for j in range(n_per_step):
        in_specs.append(pl.BlockSpec((page * 2 * A_HEADS, A_DV), lambda s, p, pt, f=page_id(j): (f(s, p, pt), 0)))
    for j in range(n_per_step):
        in_specs.append(pl.BlockSpec((1, mla_row, page), lambda s, p, pt, f=page_id(j): (f(s, p, pt), 0, 0)))
    kern = functools.partial(_sample_attn_kernel, lam_init=lam_init, kv_lora=kv_lora, n_per_step=n_per_step)
    grid_spec = pltpu.PrefetchScalarGridSpec(
        num_scalar_prefetch=1,
        grid=(sb, steps),
        in_specs=in_specs,
        out_specs=pl.BlockSpec((1, dec, mix_w), lambda s, p, pt: (s, 0, 0)),
        scratch_shapes=[pltpu.VMEM((rows_d, A_WIDTH), BF16), pltpu.VMEM((rows_m, mla_row), BF16),
                        pltpu.VMEM((rows_d, 1), F32), pltpu.VMEM((rows_d, 1), F32),
                        pltpu.VMEM((rows_d, A_WIDTH), F32),
                        pltpu.VMEM((rows_m, 1), F32), pltpu.VMEM((rows_m, 1), F32),
                        pltpu.VMEM((rows_m, kv_lora), F32)],
    )
    return pl.pallas_call(
        kern,
        grid_spec=grid_spec,
        out_shape=jax.ShapeDtypeStruct((sb, dec, mix_w), BF16),
        compiler_params=_cparams(("parallel", "arbitrary")),
        name="sample_attn",
    )(pt_flat, qd, qm, kdn, vdn, kmn, bias, biasn, maskn, lam_p, subg, wuv,
      *([cache_kv] * n_per_step), *([cache_mla_t] * n_per_step))


def _layer_norm(x, g, b):
    xc = x - jnp.mean(x, axis=-1, keepdims=True)
    var = jnp.mean(xc * xc, axis=-1, keepdims=True)
    return xc * lax.rsqrt(var + 1e-5) * g + b


def _extract_topk(s, k_top, want_rank):
    n, t = s.shape
    iota = lax.broadcasted_iota(jnp.int32, (n, t), 0).astype(F32)
    iota_k = lax.broadcasted_iota(jnp.int32, (k_top, t), 0)
    rank = jnp.full((n, t), float(n), F32) if want_rank else None
    vals = jnp.zeros((k_top, t), F32)
    picks = []
    for k in range(k_top):
        m = jnp.max(s, axis=0, keepdims=True)
        idx = jnp.min(jnp.where(s == m, iota, float(n)), axis=0, keepdims=True)
        sel = iota == idx
        if want_rank:
            rank = jnp.where(sel, float(k), rank)
        s = jnp.where(sel, -jnp.inf, s)
        vals = jnp.where(iota_k == k, m, vals)
        picks.append(idx)
    return rank, vals, picks, s


def _post_select_kernel(mix_ref, x_ref, wout_ref, g_ref, b_ref, wq_ref, sk_ref,
                        h_out, hb_out, a_out, cnt_out, b_out, r2_out, *, alpha, mix_transposed):
    a = _tn(mix_ref[...], wout_ref[...]) if mix_transposed else _dot(mix_ref[...], wout_ref[...])
    h = _layer_norm(alpha * x_ref[...] + a, g_ref[...], b_ref[...])
    h_out[...] = h
    hb = h.astype(BF16)
    hb_out[...] = hb
    q = _dot(hb, wq_ref[...])
    half = PEER_DK // 2
    kk = PEER_TOPK
    sub = kk // 2
    iota_n = lax.broadcasted_iota(jnp.int32, (N_KEYS, q.shape[0]), 0).astype(F32)
    for hd in range(PEER_HEADS):
        q1 = q[:, hd * PEER_DK:hd * PEER_DK + half].astype(BF16)
        q2 = q[:, hd * PEER_DK + half:(hd + 1) * PEER_DK].astype(BF16)
        s1 = _nt(sk_ref[0], q1)
        s2 = _nt(sk_ref[1], q2)
        _, v1, picks1, s1_left = _extract_topk(s1, kk, False)
        r2, v2, _, _ = _extract_topk(s2, kk, True)
        cand = jnp.concatenate([v1[0:1] + v2] + [v1[k:k + 1] + v2[:sub] for k in range(1, kk)], axis=0)
        _, _, _, cand_left = _extract_topk(cand, kk, False)
        picked = cand_left == -jnp.inf
        e = jnp.where(picked, jnp.exp(cand - (v1[0:1] + v2[0:1])), 0.0)
        z = jnp.sum(e, axis=0, keepdims=True)
        pf = picked.astype(F32)
        cnt = jnp.zeros_like(s1)
        for k in range(kk):
            rows = pf[0:kk] if k == 0 else pf[kk + (k - 1) * sub:kk + k * sub]
            c_k = jnp.sum(rows, axis=0, keepdims=True)
            cnt = jnp.where(iota_n == picks1[k], c_k, cnt)
        a_out[hd] = jnp.where(s1_left == -jnp.inf, jnp.exp(s1 - v1[0:1]), 0.0) / z
        cnt_out[hd] = cnt
        b_out[hd] = jnp.where(r2 < float(kk), jnp.exp(s2 - v2[0:1]), 0.0).astype(b_out.dtype)
        r2_out[hd] = r2.astype(r2_out.dtype)


def _post_select(mix, x2d, wout, g, b, wq, sk, *, tb, alpha, mix_transposed):
    t, d = x2d.shape
    row = lambda w: pl.BlockSpec((tb, w), lambda i: (i, 0))
    mix_spec = pl.BlockSpec((mix.shape[0], tb), lambda i: (0, i)) if mix_transposed else row(mix.shape[1])
    sel = pl.BlockSpec((PEER_HEADS, N_KEYS, tb), lambda i: (0, 0, i))
    sel_shape = jax.ShapeDtypeStruct((PEER_HEADS, N_KEYS, t), F32)
    sel_half = jax.ShapeDtypeStruct((PEER_HEADS, N_KEYS, t), BF16)
    return pl.pallas_call(
        functools.partial(_post_select_kernel, alpha=alpha, mix_transposed=mix_transposed),
        grid=(t // tb,),
        in_specs=[mix_spec, row(d), _full(wout.shape), _full(g.shape), _full(b.shape),
                  _full(wq.shape), _full(sk.shape)],
        out_specs=[row(d), row(d), sel, sel, sel, sel],
        out_shape=[jax.ShapeDtypeStruct((t, d), F32), jax.ShapeDtypeStruct((t, d), BF16),
                   sel_shape, sel_shape, sel_half, sel_half],
        compiler_params=_cparams(("parallel",)),
        name="post_select",
    )(mix, x2d, wout, g, b, wq, sk)


def _peer_dense_kernel(h_ref, hb_ref, a_ref, cnt_ref, b_ref, r2_ref, u_ref, v_ref, g_ref, bb_ref,
                       o_ref, acc, s_scr, w_scr, *, alpha, rows_per_step):
    e = pl.program_id(1)

    @pl.when(e == 0)
    def _():
        acc[...] = jnp.zeros(acc.shape, F32)

    s_scr[...] = _nt(u_ref[...], hb_ref[...])

    tb = hb_ref.shape[0]
    pack = 2 * SUBLANES
    spread = lambda row: jnp.concatenate([jnp.broadcast_to(row, (pack, tb)).astype(BF16)] * (N_KEYS // pack), axis=0)

    def row_body(r, carry):
        i = e * rows_per_step + r
        rows = pl.ds(pl.multiple_of(r * N_KEYS, N_KEYS), N_KEYS)
        s = s_scr[rows, :]
        act = (0.5 * s * (1.0 + lax.erf(s * (2.0 ** -0.5)))).astype(BF16)
        zero = jnp.zeros((N_KEYS, tb), BF16)
        gate = zero
        for hd in range(PEER_HEADS):
            a_b = spread(a_ref[hd, pl.ds(i, 1), :])
            c_b = spread(cnt_ref[hd, pl.ds(i, 1), :])
            gate = gate + jnp.where(r2_ref[hd] < c_b, b_ref[hd], zero) * a_b
        w_scr[rows, :] = gate * act
        return carry

    lax.fori_loop(0, rows_per_step, row_body, 0)
    acc[...] += _tn(w_scr[...], v_ref[...])

    @pl.when(e == pl.num_programs(1) - 1)
    def _():
        o_ref[...] = _layer_norm(alpha * h_ref[...] + acc[...], g_ref[...], bb_ref[...])


def _peer_dense(h, hb, sel_a, sel_cnt, sel_b, sel_r2, u_tab, v_tab, g, b, *, tb, rows_per_step, alpha):
    t, d = h.shape
    n_exp = u_tab.shape[0]
    eb = rows_per_step * N_KEYS
    row = lambda w: pl.BlockSpec((tb, w), lambda i, e: (i, 0))
    sel = pl.BlockSpec((PEER_HEADS, N_KEYS, tb), lambda i, e: (0, 0, i))
    tab = pl.BlockSpec((eb, d), lambda i, e: (e, 0))
    cst = lambda shape: pl.BlockSpec(shape, lambda i, e: (0,) * len(shape))
    return pl.pallas_call(
        functools.partial(_peer_dense_kernel, alpha=alpha, rows_per_step=rows_per_step),
        grid=(t // tb, n_exp // eb),
        in_specs=[row(d), row(d), sel, sel, sel, sel, tab, tab, cst(g.shape), cst(b.shape)],
        out_specs=row(d),
        out_shape=jax.ShapeDtypeStruct((t, d), F32),
        scratch_shapes=[pltpu.VMEM((tb, d), F32), pltpu.VMEM((eb, tb), F32), pltpu.VMEM((eb, tb), BF16)],
        compiler_params=_cparams(("parallel", "arbitrary")),
        name="peer_dense",
    )(h, hb, sel_a, sel_cnt, sel_b, sel_r2, u_tab, v_tab, g, b)


def _rope_tables(pos):
    inv = 1.0 / (ROPE_THETA ** (jnp.arange(0, B_DR, 2, dtype=F32) / B_DR))
    ang = pos.astype(F32)[:, None] * inv[None, :]
    cos, sin = jnp.cos(ang), jnp.sin(ang)
    pad = jnp.zeros((pos.shape[0], LANES - B_DR), F32)
    return (jnp.concatenate([cos, cos, pad], axis=-1), jnp.concatenate([-sin, sin, pad], axis=-1))


def _swap_halves(w):
    half = w.shape[-1] // 2
    return jnp.concatenate([w[..., half:], w[..., :half]], axis=-1)


def _pad_lanes(w):
    return jnp.pad(w, [(0, 0)] * (w.ndim - 1) + [(0, LANES - w.shape[-1])])


def _pick(n, prefs):
    for c in prefs:
        if n % c == 0:
            return c
    return n


def _rel_bias_values(rel_bias, rel):
    onehot = (_t5_bucket(rel)[..., None] == jnp.arange(REL_BUCKETS)).astype(F32)
    vals = jnp.einsum('...b,bh->h...', onehot, rel_bias.astype(F32), precision=lax.Precision.HIGHEST)
    return vals * LOG2E


def kernel(x_prompt, x_sample, cache_diff_kv, cache_mla, page_table, w_in, diff_lambda, diff_subln_g, rel_bias,
           mla_q_norm_g, mla_w_uq, mla_kv_norm_g, mla_w_uk, mla_w_uv, w_out, ln1_g, ln1_b, peer_w_q,
           peer_sub_keys, peer_u, peer_v, ln2_g, ln2_b):
    batch, seq, d = x_prompt.shape
    sb, dec, _ = x_sample.shape
    depth = w_in.shape[0]
    n_pool, page = cache_mla.shape[1], cache_mla.shape[2]
    n_pages = page_table.shape[1]
    past = n_pages * page
    q_lora = mla_q_norm_g.shape[1]
    kv_lora = mla_kv_norm_g.shape[1]
    mla_row = kv_lora + B_DR
    alpha = (2 * depth) ** 0.25
    aw = A_WIDTH

    pos_p = jnp.arange(seq, dtype=jnp.int32)
    pos_s = past + jnp.arange(dec, dtype=jnp.int32)
    cos_p, sin_p = _rope_tables(jnp.tile(pos_p, batch))
    cos_s, sin_s = _rope_tables(jnp.tile(pos_s, sb))

    rel_sat = REL_EXACT * (REL_MAX_DIST / REL_EXACT) ** ((REL_BUCKETS - 1 - REL_EXACT) / (REL_BUCKETS - REL_EXACT))
    bq = _pick(seq, (512, 256, 128))
    bk = _pick(seq, (256, 128))
    ratio = bq // bk
    assert rel_sat < page

    rel_t = jnp.arange(bq, dtype=jnp.int32)[None, :] - jnp.arange(bk, dtype=jnp.int32)[:, None]
    tiles = []
    t = 0
    while True:
        off = (t - (ratio - 1)) * bk
        rel = rel_t + off
        ok = rel >= 0
        tiles.append(jnp.concatenate([jnp.where(ok[None], _rel_bias_values(rel_bias, rel), NEG),
                                      jnp.where(ok, 0.0, NEG)[None]], axis=0))
        if off - (bk - 1) >= rel_sat:
            break
        t += 1
    bias_p = jnp.stack(tiles)

    n_per_step = _pick(n_pages, (16, 8, 4, 2, 1))
    rows_d = 2 * A_HEADS * dec
    rep = lambda v: jnp.broadcast_to(v[:, None], (A_HEADS, 2) + v.shape[1:]).reshape((rows_d,) + v.shape[2:])
    far_s = jnp.broadcast_to(_rel_bias_values(rel_bias, jnp.full((1, 1), past, jnp.int32)),
                             (A_HEADS, dec, n_per_step * page))
    kpos_last = past - page + jnp.arange(page, dtype=jnp.int32)
    last = _rel_bias_values(rel_bias, pos_s[:, None] - kpos_last[None, :])
    bias_s = jnp.stack([rep(far_s), rep(jnp.concatenate([far_s[..., :(n_per_step - 1) * page], last], axis=-1))])
    rd = jnp.arange(dec, dtype=jnp.int32)
    rel_n = rd[:, None] - rd[None, :]
    bias_n = rep(jnp.where(rel_n[None] >= 0, _rel_bias_values(rel_bias, rel_n), NEG))
    mask_n = jnp.tile(jnp.where(rel_n >= 0, 0.0, NEG).astype(F32), (B_HEADS, 1))

    cache_kv2 = cache_diff_kv.reshape(depth * n_pool * page * 2 * A_HEADS, A_DV)
    cache_mla_t = jnp.swapaxes(cache_mla, 2, 3).reshape(depth * n_pool, mla_row, page)

    tm = _pick(seq, (512, 256, 128))
    tm_s = _pick(sb * dec, (512, 256, 128))
    tb = _pick(batch * seq, (512, 256, 128))
    tb_s = _pick(sb * dec, (512, 256, 128))

    hp = x_prompt.reshape(batch * seq, d)
    hs = x_sample.reshape(sb * dec, d)
    kv_p, mla_p, kv_s, mla_s = [], [], [], []
    for l in range(depth):
        lam_init = 0.8 - 0.6 * math.exp(-0.3 * l)
        w = w_in[l]
        o = 3 * aw + q_lora + kv_lora
        kr_w = w[:, o:]
        win = jnp.concatenate([w[:, :o], _pad_lanes(kr_w), _pad_lanes(_swap_halves(kr_w))], axis=-1).astype(BF16)
        uq = mla_w_uq[l]
        uq_r = uq[..., B_DN:]
        wuq = jnp.concatenate([uq[..., :B_DN].reshape(q_lora, -1), _pad_lanes(uq_r).reshape(q_lora, -1),
                               _pad_lanes(_swap_halves(uq_r)).reshape(q_lora, -1)], axis=-1).astype(BF16)
        wuk = jnp.transpose(mla_w_uk[l], (1, 2, 0)).astype(BF16)
        wuv = jnp.transpose(mla_w_uv[l], (1, 0, 2)).astype(BF16)
        wuvt = jnp.transpose(mla_w_uv[l], (1, 2, 0)).astype(BF16)
        qg = mla_q_norm_g[l].reshape(1, -1)
        kvg = mla_kv_norm_g[l].reshape(1, -1)
        subg = diff_subln_g[l].reshape(1, -1)
        subg_col = diff_subln_g[l].reshape(-1, 1)
        lam_p = diff_lambda[l]
        wout = w_out[l].astype(BF16)
        wq = peer_w_q[l].astype(BF16)
        sk = peer_sub_keys[l].astype(BF16)
        u_tab = peer_u[l].astype(BF16)
        v_tab = peer_v[l].astype(BF16)
        g1, b1 = ln1_g[l].reshape(1, -1), ln1_b[l].reshape(1, -1)
        g2, b2 = ln2_g[l].reshape(1, -1), ln2_b[l].reshape(1, -1)

        def post(h2d, mix, tb_, mix_transposed):
            h1, h1b, sa, sc, sbb, sr = _post_select(mix, h2d, wout, g1, b1, wq, sk, tb=tb_, alpha=alpha,
                                                    mix_transposed=mix_transposed)
            return _peer_dense(h1, h1b, sa, sc, sbb, sr, u_tab, v_tab, g2, b2, tb=tb_, rows_per_step=8,
                               alpha=alpha)

        kv_rows, mla_t, qd, kd, _, vdt, qm, km, ct = _project(
            hp, cos_p, sin_p, win, qg, wuq, kvg, wuk, groups=batch, tm=tm, q_lora=q_lora, kv_lora=kv_lora)
        mix_t = _prompt_attn(qd, qm, kd, vdt, km, ct, bias_p, lam_p, subg_col, wuvt, batch=batch, seq=seq,
                             bq=bq, bk=bk, lam_init=lam_init, kv_lora=kv_lora)
        kv_p.append(kv_rows.reshape(batch, seq, 2, A_HEADS, A_DV))
        mla_p.append(jnp.swapaxes(mla_t, 1, 2))
        hp = post(hp, mix_t, tb, True)

        kv_rows, mla_t, qd, kd, vd, _, qm, km, _ = _project(
            hs, cos_s, sin_s, win, qg, wuq, kvg, wuk, groups=1, tm=tm_s, q_lora=q_lora, kv_lora=kv_lora)
        r3 = lambda a: a.reshape(sb, dec, a.shape[-1])
        mix = _sample_attn(page_table, r3(qd), r3(qm), r3(kd), r3(vd), r3(km), cache_kv2, cache_mla_t,
                           bias_s, bias_n, mask_n, lam_p, subg, wuv, layer_off=l * n_pool, lam_init=lam_init,
                           kv_lora=kv_lora, n_per_step=n_per_step)
        kv_s.append(kv_rows.reshape(sb, dec, 2, A_HEADS, A_DV))
        mla_s.append(jnp.swapaxes(mla_t, 1, 2).reshape(sb, dec, mla_row))
        hs = post(hs, mix.reshape(sb * dec, -1), tb_s, False)

    return (hp.reshape(batch, seq, d), hs.reshape(sb, dec, d), jnp.stack(kv_p), jnp.stack(mla_p),
            jnp.stack(kv_s), jnp.stack(mla_s))
```
